```python
import math
import jax, jax.numpy as jnp
from jax import lax
import numpy as np

D_MODEL = 2048
BATCH = 8
SEQ = 2048
DEPTH = 1

HEAD_DIM = 64
LRU_WIDTH = D_MODEL // 2
LRU_HEADS = LRU_WIDTH // HEAD_DIM
LRU_HEAD_DIM = HEAD_DIM
ATTN_WIDTH = D_MODEL - LRU_WIDTH
ATTN_HEADS = ATTN_WIDTH // HEAD_DIM
ATTN_HEAD_DIM = HEAD_DIM
MIX_WIDTH = LRU_WIDTH + ATTN_WIDTH
IN_PROJ_WIDTH = 2 * LRU_WIDTH + 3 * ATTN_WIDTH
CONV_WIDTH = 4
RG_C = 8.0
GRID_W = 64
WIN_ROWS = 8
WIN_COLS = 16
D_FF = 256 * math.ceil(8 * D_MODEL / 3 / 256)
NORM_EPS = 1e-6

kernel_name = "hybrid_rglru_natten_macaron"


def rms_norm(x, g):
    xf = x.astype(jnp.float32)
    y = xf * lax.rsqrt(jnp.mean(xf * xf, axis=-1, keepdims=True) + NORM_EPS) * g.astype(jnp.float32)
    return y.astype(x.dtype)


def swiglu(u, w_in, w_out):
    gate, up = jnp.split(u @ w_in, 2, axis=-1)
    return (jax.nn.silu(gate) * up) @ w_out


def _lin_rec_combine(left, right):
    a_l, b_l = left
    a_r, b_r = right
    return a_l * a_r, a_r * b_l + b_r


def rglru_mixer(x_in, gate_in, conv_w, conv_b, gate_w, gate_b, lam):
    B, S, C = x_in.shape
    pad_l = CONV_WIDTH // 2
    xc = lax.conv_general_dilated(
        x_in, conv_w[:, None, :], window_strides=(1,),
        padding=[(pad_l, CONV_WIDTH - 1 - pad_l)],
        dimension_numbers=("NWC", "WIO", "NWC"), feature_group_count=C) + conv_b
    xf = xc.astype(jnp.float32)
    xh = xf.reshape(B, S, LRU_HEADS, LRU_HEAD_DIM)
    gates = jax.nn.sigmoid(
        jnp.einsum("bshi,zghij->zgbshj", xh, gate_w.astype(jnp.float32))
        + gate_b.astype(jnp.float32)[:, :, None, None])
    gates = gates.reshape(2, 2, B, S, C)
    r, i = gates[:, 0], gates[:, 1]
    log_a = -RG_C * r * jax.nn.softplus(-lam.astype(jnp.float32))[:, None, None, :]
    a = jnp.exp(log_a)
    b = jnp.sqrt(-jnp.expm1(2.0 * log_a)) * (i * xf)
    h_fwd = lax.associative_scan(_lin_rec_combine, (a[0], b[0]), axis=1)[1]
    h_bwd = lax.associative_scan(_lin_rec_combine, (a[1], b[1]), axis=1, reverse=True)[1]
    y = jax.nn.gelu(gate_in.astype(jnp.float32)) * (h_fwd + h_bwd)
    return y.astype(x_in.dtype)


def neighbourhood_attention(q, k, v, rpb):
    B, S, _ = q.shape
    rows = S // GRID_W
    kr = min(WIN_ROWS, rows)
    grid = (B, rows, GRID_W, ATTN_HEADS, ATTN_HEAD_DIM)
    scale = ATTN_HEAD_DIM ** -0.5
    qg = q.reshape(grid).astype(jnp.float32) * scale
    kg = k.reshape(grid).astype(jnp.float32)
    vg = v.reshape(grid).astype(jnp.float32)
    cols = jnp.arange(GRID_W)
    col_start = jnp.clip(cols - WIN_COLS // 2, 0, GRID_W - WIN_COLS)
    col_idx = col_start[:, None] + jnp.arange(WIN_COLS)
    sel = jax.nn.one_hot(col_idx, GRID_W, dtype=jnp.float32)
    col_bias = rpb.astype(jnp.float32)[:, :, col_idx - cols[:, None] + WIN_COLS - 1]

    def row_block(r):
        rs = jnp.clip(r - kr // 2, 0, rows - kr)
        q_row = lax.dynamic_index_in_dim(qg, r, axis=1, keepdims=False)
        k_band = lax.dynamic_slice_in_dim(kg, rs, kr, axis=1)
        v_band = lax.dynamic_slice_in_dim(vg, rs, kr, axis=1)
        s_band = jnp.einsum("bqhd,brkhd->bhqrk", q_row, k_band)
        s = jnp.einsum("bhqrk,qwk->bhqrw", s_band, sel)
        row_off = rs + jnp.arange(kr) - r + WIN_ROWS - 1
        bias = jnp.take(col_bias, row_off, axis=1).transpose(0, 2, 1, 3)
        s = s + bias[None]
        p = jax.nn.softmax(s.reshape(B, ATTN_HEADS, GRID_W, kr * WIN_COLS), axis=-1)
        p = p.reshape(B, ATTN_HEADS, GRID_W, kr, WIN_COLS)
        p_band = jnp.einsum("bhqrw,qwk->bhqrk", p, sel)
        return jnp.einsum("bhqrk,brkhd->bqhd", p_band, v_band)

    out = lax.map(row_block, jnp.arange(rows))
    return out.transpose(1, 0, 2, 3, 4).reshape(B, S, ATTN_WIDTH).astype(q.dtype)


def setup_inputs(seed: int = 0) -> dict:
    key = jax.random.key(seed)
    ks = jax.random.split(key, 24)
    f32 = jnp.float32
    L = DEPTH

    def nrm(k, shape, scale):
        return jax.random.normal(k, shape, f32) * scale

    def gain(k, shape):
        return 1.0 + 0.05 * jax.random.normal(k, shape, f32)

    a_c = jax.random.uniform(ks[10], (L, 2, LRU_WIDTH), f32, 0.9, 0.999)
    a_base = a_c ** (1.0 / RG_C)
    lru_lambda = jnp.log(a_base) - jnp.log1p(-a_base)
    return {
        "x": nrm(ks[0], (BATCH, SEQ, D_MODEL), 1.0),
        "norm_ffn1": gain(ks[1], (L, D_MODEL)),
        "ffn1_w_in": nrm(ks[2], (L, D_MODEL, 2 * D_FF), D_MODEL ** -0.5),
        "ffn1_w_out": nrm(ks[3], (L, D_FF, D_MODEL), D_FF ** -0.5),
        "norm_mix": gain(ks[4], (L, D_MODEL)),
        "w_in_mix": nrm(ks[5], (L, D_MODEL, IN_PROJ_WIDTH), D_MODEL ** -0.5),
        "lru_conv_w": nrm(ks[6], (L, CONV_WIDTH, LRU_WIDTH), CONV_WIDTH ** -0.5),
        "lru_conv_b": nrm(ks[7], (L, LRU_WIDTH), 0.01),
        "lru_gate_w": nrm(ks[8], (L, 2, 2, LRU_HEADS, LRU_HEAD_DIM, LRU_HEAD_DIM), LRU_HEAD_DIM ** -0.5),
        "lru_gate_b": nrm(ks[9], (L, 2, 2, LRU_HEADS, LRU_HEAD_DIM), 0.01),
        "lru_lambda": lru_lambda,
        "attn_rpb": nrm(ks[11], (L, ATTN_HEADS, 2 * WIN_ROWS - 1, 2 * WIN_COLS - 1), 0.02),
        "lru_out_norm": gain(ks[12], (L, LRU_WIDTH)),
        "attn_out_norm": gain(ks[13], (L, ATTN_WIDTH)),
        "w_out_mix": nrm(ks[14], (L, MIX_WIDTH, D_MODEL), MIX_WIDTH ** -0.5),
        "norm_ffn2": gain(ks[15], (L, D_MODEL)),
        "ffn2_w_in": nrm(ks[16], (L, D_MODEL, 2 * D_FF), D_MODEL ** -0.5),
        "ffn2_w_out": nrm(ks[17], (L, D_FF, D_MODEL), D_FF ** -0.5),
        "norm_final": gain(ks[18], (D_MODEL,)),
    }


def reference(x, norm_ffn1, ffn1_w_in, ffn1_w_out, norm_mix, w_in_mix, lru_conv_w, lru_conv_b,
              lru_gate_w, lru_gate_b, lru_lambda, attn_rpb, lru_out_norm, attn_out_norm,
              w_out_mix, norm_ffn2, ffn2_w_in, ffn2_w_out, norm_final):
    h = x
    for l in range(DEPTH):
        h = h + 0.5 * swiglu(rms_norm(h, norm_ffn1[l]), ffn1_w_in[l], ffn1_w_out[l])
        u = rms_norm(h, norm_mix[l])
        proj = u @ w_in_mix[l]
        x_lru, g_lru, q, k, v = jnp.split(
            proj, [LRU_WIDTH, 2 * LRU_WIDTH, 2 * LRU_WIDTH + ATTN_WIDTH,
                   2 * LRU_WIDTH + 2 * ATTN_WIDTH], axis=-1)
        y_a = rglru_mixer(x_lru, g_lru, lru_conv_w[l], lru_conv_b[l], lru_gate_w[l],
                          lru_gate_b[l], lru_lambda[l])
        y_b = neighbourhood_attention(q, k, v, attn_rpb[l])
        y = jnp.concatenate([rms_norm(y_a, lru_out_norm[l]), rms_norm(y_b, attn_out_norm[l])], axis=-1)
        h = h + y @ w_out_mix[l]
        h = h + 0.5 * swiglu(rms_norm(h, norm_ffn2[l]), ffn2_w_in[l], ffn2_w_out[l])
    return rms_norm(h, norm_final)
```

```python
import functools
import math

import jax
import jax.numpy as jnp
from jax import lax
from jax.experimental import pallas as pl
from jax.experimental.pallas import tpu as pltpu

D_MODEL = 2048
HEAD_DIM = 64
LRU_WIDTH = 1024
ATTN_WIDTH = 1024
ATTN_HEADS = ATTN_WIDTH // HEAD_DIM
IN_PROJ_WIDTH = 2 * LRU_WIDTH + 3 * ATTN_WIDTH
CONV_WIDTH = 4
CONV_PAD_L = CONV_WIDTH // 2
RG_C = 8.0
GRID_W = 64
WIN_ROWS = 8
WIN_COLS = 16
D_FF = 5632
NORM_EPS = 1e-6

LANES = 128
SUBLANES = 8
MASK_VALUE = -1e30

FFN_TM = 512
FFN_TF = 512
PROJ_TM = 1024
PROJ_TN = 1024
OUT_TM = 512
LRU_TB = 256
SCAN_UNROLL = 8
TIME_PAD = SUBLANES


def _vmem(mib):
    return pltpu.CompilerParams(
        dimension_semantics=("parallel", "arbitrary"), vmem_limit_bytes=mib * 1024 * 1024)


def _rms_norm(x, gain):
    ms = jnp.mean(x * x, axis=-1, keepdims=True)
    return x * lax.rsqrt(ms + NORM_EPS) * gain


def _ffn_kernel(x_ref, g_ref, wg_ref, wu_ref, wo_ref, gf_ref, o_ref, u_ref, *, final_norm):
    k = pl.program_id(1)

    @pl.when(k == 0)
    def _():
        x = x_ref[...]
        u_ref[...] = _rms_norm(x, g_ref[...]).astype(jnp.bfloat16)
        o_ref[...] = x

    u = u_ref[...]
    gate = jnp.dot(u, wg_ref[...], preferred_element_type=jnp.float32)
    up = jnp.dot(u, wu_ref[...], preferred_element_type=jnp.float32)
    act = (gate * jax.nn.sigmoid(gate)) * (0.5 * up)
    o_ref[...] += jnp.dot(act.astype(jnp.bfloat16), wo_ref[...],
                          preferred_element_type=jnp.float32)

    if final_norm:
        @pl.when(k == pl.num_programs(1) - 1)
        def _():
            o_ref[...] = _rms_norm(o_ref[...], gf_ref[...])


def _ffn(h, gain, w_in, w_out, final_gain, final_norm):
    t, d = h.shape
    nk = D_FF // FFN_TF
    return pl.pallas_call(
        functools.partial(_ffn_kernel, final_norm=final_norm),
        grid=(t // FFN_TM, nk),
        in_specs=[
            pl.BlockSpec((FFN_TM, d), lambda i, k: (i, 0)),
            pl.BlockSpec((1, d), lambda i, k: (0, 0)),
            pl.BlockSpec((d, FFN_TF), lambda i, k: (0, k)),
            pl.BlockSpec((d, FFN_TF), lambda i, k: (0, k + nk)),
            pl.BlockSpec((FFN_TF, d), lambda i, k: (k, 0)),
            pl.BlockSpec((1, d), lambda i, k: (0, 0)),
        ],
        out_specs=pl.BlockSpec((FFN_TM, d), lambda i, k: (i, 0)),
        out_shape=jax.ShapeDtypeStruct((t, d), jnp.float32),
        scratch_shapes=[pltpu.VMEM((FFN_TM, d), jnp.bfloat16)],
        compiler_params=_vmem(48),
        name="ffn_final" if final_norm else "ffn",
    )(h, gain, w_in, w_in, w_out, final_gain)


def _in_proj_kernel(x_ref, g_ref, w_ref, o_ref, u_ref):
    @pl.when(pl.program_id(1) == 0)
    def _():
        u_ref[...] = _rms_norm(x_ref[...], g_ref[...]).astype(jnp.bfloat16)

    o_ref[...] = jnp.dot(u_ref[...], w_ref[...], preferred_element_type=jnp.float32)


def _in_proj(h, gain, w):
    t, d = h.shape
    n = w.shape[1]
    return pl.pallas_call(
        _in_proj_kernel,
        grid=(t // PROJ_TM, n // PROJ_TN),
        in_specs=[
            pl.BlockSpec((PROJ_TM, d), lambda i, j: (i, 0)),
            pl.BlockSpec((1, d), lambda i, j: (0, 0)),
            pl.BlockSpec((d, PROJ_TN), lambda i, j: (0, j)),
        ],
        out_specs=pl.BlockSpec((PROJ_TM, PROJ_TN), lambda i, j: (i, j)),
        out_shape=jax.ShapeDtypeStruct((t, n), jnp.float32),
        scratch_shapes=[pltpu.VMEM((PROJ_TM, d), jnp.bfloat16)],
        compiler_params=_vmem(48),
        name="in_proj",
    )(h, gain, w)


def _lru_kernel(*refs, reverse, n_chunks):
    if reverse:
        x_ref, w_ref, gb_ref, cw_ref, cb_ref, lam_ref, hf_ref, g_ref, o_ref, a_scr, b_scr, h_scr = refs
    else:
        x_ref, w_ref, gb_ref, cw_ref, cb_ref, lam_ref, o_ref, a_scr, b_scr, h_scr = refs
    j = pl.program_id(1)
    chunk = (n_chunks - 1 - j) if reverse else j
    t0 = pl.multiple_of(chunk * LRU_TB, LRU_TB)

    @pl.when(j == 0)
    def _():
        h_scr[...] = jnp.zeros_like(h_scr)

    xc = cb_ref[...][None]
    for tap in range(CONV_WIDTH):
        xc = xc + cw_ref[tap:tap + 1, :][None] * x_ref[pl.ds(t0 + tap, LRU_TB)]
    rows = LRU_TB * SUBLANES
    xc2 = xc.reshape(rows, LANES)
    pre = jnp.dot(xc2.astype(jnp.bfloat16), w_ref[0], preferred_element_type=jnp.float32)
    pre = pre + gb_ref[0]
    r = jax.nn.sigmoid(pre[:, :LANES])
    i = jax.nn.sigmoid(pre[:, LANES:])
    z = -lam_ref[0]
    softplus = jnp.maximum(z, 0.0) + jnp.log1p(jnp.exp(-jnp.abs(z)))
    log_a = (-RG_C * softplus) * r
    a = jnp.exp(log_a)
    b = jnp.sqrt(-jnp.tanh(log_a) * (a * a + 1.0)) * (i * xc2)
    a_scr[...] = a.reshape(LRU_TB, SUBLANES, LANES)
    b_scr[...] = b.reshape(LRU_TB, SUBLANES, LANES)

    def step(s, h):
        t = (LRU_TB - 1 - s) if reverse else s
        h = a_scr[t] * h + b_scr[t]
        o_ref[t] = h
        return h

    h_scr[...] = lax.fori_loop(0, LRU_TB, step, h_scr[...], unroll=SCAN_UNROLL)

    if reverse:
        o_ref[...] = jax.nn.gelu(g_ref[...], approximate=True) * (hf_ref[...] + o_ref[...])


def _lru_direction(xs, gate_w, gate_b, conv_w, conv_b, lam, hf, gs, reverse):
    s_pad, bsz, c = xs.shape
    s = s_pad - TIME_PAD
    n_cblk = c // LANES
    n_chunks = s // LRU_TB
    chunk_of = (lambda j: n_chunks - 1 - j) if reverse else (lambda j: j)
    chunk_spec = pl.BlockSpec((LRU_TB, bsz, LANES), lambda cb, j: (chunk_of(j), 0, cb))
    in_specs = [
        pl.BlockSpec((s_pad, bsz, LANES), lambda cb, j: (0, 0, cb)),
        pl.BlockSpec((1, LANES, 2 * LANES), lambda cb, j: (cb, 0, 0)),
        pl.BlockSpec((1, 1, 2 * LANES), lambda cb, j: (cb, 0, 0)),
        pl.BlockSpec((CONV_WIDTH, LANES), lambda cb, j: (0, cb)),
        pl.BlockSpec((1, LANES), lambda cb, j: (0, cb)),
        pl.BlockSpec((1, 1, LANES), lambda cb, j: (0, 0, cb)),
    ]
    args = [xs, gate_w, gate_b, conv_w, conv_b, lam]
    if reverse:
        in_specs += [chunk_spec, chunk_spec]
        args += [hf, gs]
    return pl.pallas_call(
        functools.partial(_lru_kernel, reverse=reverse, n_chunks=n_chunks),
        grid=(n_cblk, n_chunks),
        in_specs=in_specs,
        out_specs=chunk_spec,
        out_shape=jax.ShapeDtypeStruct((s, bsz, c), jnp.float32),
        scratch_shapes=[
            pltpu.VMEM((LRU_TB, bsz, LANES), jnp.float32),
            pltpu.VMEM((LRU_TB, bsz, LANES), jnp.float32),
            pltpu.VMEM((bsz, LANES), jnp.float32),
        ],
        compiler_params=_vmem(48),
        name="lru_bwd" if reverse else "lru_fwd",
    )(*args)


def _lru_gate_weights(gate_w, gate_b):
    n_cblk = LRU_WIDTH // LANES
    hpb = LANES // HEAD_DIM
    w = gate_w.reshape(2, 2, n_cblk, hpb, HEAD_DIM, HEAD_DIM)
    eye = jnp.eye(hpb, dtype=gate_w.dtype)
    wbd = w[:, :, :, :, :, None, :] * eye[None, None, None, :, None, :, None]
    wbd = wbd.reshape(2, 2, n_cblk, LANES, LANES)
    wbd = wbd.transpose(0, 2, 3, 1, 4).reshape(2, n_cblk, LANES, 2 * LANES)
    bias = gate_b.reshape(2, 2, n_cblk, LANES).transpose(0, 2, 1, 3).reshape(2, n_cblk, 1, 2 * LANES)
    return wbd.astype(jnp.bfloat16), bias


def _attn_kernel(q_ref, k_ref, v_ref, bias_ref, o_ref, q_scr, k_scr, v_scr, *, rows, kr):
    lane = lax.broadcasted_iota(jnp.int32, (1, LANES), 1)
    scale = HEAD_DIM ** -0.5
    q = q_ref[0] * scale
    k_scr[...] = k_ref[0].astype(jnp.bfloat16)
    v = v_ref[0]
    for hh in range(2):
        sel = (lane >= HEAD_DIM) if hh else (lane < HEAD_DIM)
        q_scr[hh] = jnp.where(sel, q, 0.0).astype(jnp.bfloat16)
        v_scr[hh] = jnp.where(sel, v, 0.0).astype(jnp.bfloat16)

    band = kr * GRID_W

    def row(r, carry):
        rs = jnp.clip(r - kr // 2, 0, rows - kr)
        off = r - rs
        q0 = pl.multiple_of(r * GRID_W, GRID_W)
        k0 = pl.multiple_of(rs * GRID_W, GRID_W)
        kb = k_scr[pl.ds(k0, band), :]
        acc = jnp.zeros((GRID_W, LANES), jnp.float32)
        for hh in range(2):
            s = lax.dot_general(q_scr[hh, pl.ds(q0, GRID_W), :], kb,
                                (((1,), (1,)), ((), ())), preferred_element_type=jnp.float32)
            s = s + bias_ref[hh, off]
            m = jnp.max(s, axis=-1, keepdims=True)
            p = jnp.exp(s - m)
            l = jnp.sum(p, axis=-1, keepdims=True)
            pv = jnp.dot(p.astype(jnp.bfloat16), v_scr[hh, pl.ds(k0, band), :],
                         preferred_element_type=jnp.float32)
            acc = acc + pv / l
        o_ref[0, pl.ds(q0, GRID_W), :] = acc
        return carry

    lax.fori_loop(0, rows, row, 0)


def _attention(proj, bias, bsz, s):
    rows = s // GRID_W
    kr = min(WIN_ROWS, rows)
    n_hp = ATTN_WIDTH // LANES
    q_blk = 2 * LRU_WIDTH // LANES
    k_blk = q_blk + n_hp
    v_blk = k_blk + n_hp
    blk = (1, s, LANES)
    return pl.pallas_call(
        functools.partial(_attn_kernel, rows=rows, kr=kr),
        grid=(bsz, n_hp),
        in_specs=[
            pl.BlockSpec(blk, lambda b, hp: (b, 0, q_blk + hp)),
            pl.BlockSpec(blk, lambda b, hp: (b, 0, k_blk + hp)),
            pl.BlockSpec(blk, lambda b, hp: (b, 0, v_blk + hp)),
            pl.BlockSpec((2, kr, GRID_W, kr * GRID_W), lambda b, hp: (hp, 0, 0, 0)),
        ],
        out_specs=pl.BlockSpec(blk, lambda b, hp: (b, 0, hp)),
        out_shape=jax.ShapeDtypeStruct((bsz, s, ATTN_WIDTH), jnp.float32),
        scratch_shapes=[
            pltpu.VMEM((2, s, LANES), jnp.bfloat16),
            pltpu.VMEM((s, LANES), jnp.bfloat16),
            pltpu.VMEM((2, s, LANES), jnp.bfloat16),
        ],
        compiler_params=_vmem(32),
        name="attn",
    )(proj, proj, proj, bias)


def _attn_bias_slabs(rpb, rows):
    kr = min(WIN_ROWS, rows)
    cols = jnp.arange(GRID_W)
    col_start = jnp.clip(cols - WIN_COLS // 2, 0, GRID_W - WIN_COLS)
    rel = cols[None, :] - cols[:, None] + WIN_COLS - 1
    inside = (cols[None, :] >= col_start[:, None]) & (cols[None, :] < col_start[:, None] + WIN_COLS)
    per_row = jnp.where(inside[None, None], rpb[:, :, jnp.clip(rel, 0, 2 * WIN_COLS - 2)], MASK_VALUE)
    row_off = jnp.arange(kr)[None, :] - jnp.arange(kr)[:, None] + WIN_ROWS - 1
    slabs = per_row[:, row_off]
    return slabs.transpose(0, 1, 3, 2, 4).reshape(rpb.shape[0], kr, GRID_W, kr * GRID_W)


def _out_proj_kernel(ya_ref, yb_ref, h_ref, ga_ref, gb_ref, wa_ref, wb_ref, o_ref):
    ua = _rms_norm(ya_ref[...], ga_ref[...]).astype(jnp.bfloat16)
    ub = _rms_norm(yb_ref[...], gb_ref[...]).astype(jnp.bfloat16)
    o_ref[...] = (h_ref[...]
                  + jnp.dot(ua, wa_ref[...], preferred_element_type=jnp.float32)
                  + jnp.dot(ub, wb_ref[...], preferred_element_type=jnp.float32))


def _out_proj(ya, yb, h, ga, gb, w):
    t, d = h.shape
    ca, cb = ya.shape[1], yb.shape[1]
    return pl.pallas_call(
        _out_proj_kernel,
        grid=(t // OUT_TM, 1),
        in_specs=[
            pl.BlockSpec((OUT_TM, ca), lambda i, j: (i, 0)),
            pl.BlockSpec((OUT_TM, cb), lambda i, j: (i, 0)),
            pl.BlockSpec((OUT_TM, d), lambda i, j: (i, 0)),
            pl.BlockSpec((1, ca), lambda i, j: (0, 0)),
            pl.BlockSpec((1, cb), lambda i, j: (0, 0)),
            pl.BlockSpec((ca, d), lambda i, j: (0, 0)),
            pl.BlockSpec((cb, d), lambda i, j: (1, 0)),
        ],
        out_specs=pl.BlockSpec((OUT_TM, d), lambda i, j: (i, 0)),
        out_shape=jax.ShapeDtypeStruct((t, d), jnp.float32),
        compiler_params=_vmem(48),
        name="out_proj",
    )(ya, yb, h, ga, gb, w, w)


def kernel(x, norm_ffn1, ffn1_w_in, ffn1_w_out, norm_mix, w_in_mix, lru_conv_w, lru_conv_b, lru_gate_w, lru_gate_b, lru_lambda, attn_rpb, lru_out_norm, attn_out_norm, w_out_mix, norm_ffn2, ffn2_w_in, ffn2_w_out, norm_final):
    bsz, s, d = x.shape
    depth = norm_ffn1.shape[0]
    assert depth >= 1 and d == D_MODEL
    bf16 = jnp.bfloat16
    t = bsz * s
    h = x.reshape(t, d)
    final_gain = norm_final.reshape(1, d)
    for l in range(depth):
        h = _ffn(h, norm_ffn1[l].reshape(1, d), ffn1_w_in[l].astype(bf16), ffn1_w_out[l].astype(bf16),
                 final_gain, final_norm=False)

        proj = _in_proj(h, norm_mix[l].reshape(1, d), w_in_mix[l].astype(bf16))
        proj = proj.reshape(bsz, s, IN_PROJ_WIDTH)

        xs = jnp.transpose(proj[:, :, :LRU_WIDTH], (1, 0, 2))
        xs = jnp.pad(xs, ((CONV_PAD_L, TIME_PAD - CONV_PAD_L), (0, 0), (0, 0)))
        gs = jnp.transpose(proj[:, :, LRU_WIDTH:2 * LRU_WIDTH], (1, 0, 2))
        gate_w, gate_b = _lru_gate_weights(lru_gate_w[l], lru_gate_b[l])
        lam = lru_lambda[l].reshape(2, 1, LRU_WIDTH)
        conv_b = lru_conv_b[l].reshape(1, LRU_WIDTH)
        hf = _lru_direction(xs, gate_w[0], gate_b[0], lru_conv_w[l], conv_b, lam[0:1], None, None, False)
        ya = _lru_direction(xs, gate_w[1], gate_b[1], lru_conv_w[l], conv_b, lam[1:2], hf, gs, True)
        ya = jnp.transpose(ya, (1, 0, 2)).reshape(t, LRU_WIDTH)

        yb = _attention(proj, _attn_bias_slabs(attn_rpb[l], s // GRID_W), bsz, s).reshape(t, ATTN_WIDTH)

        h = _out_proj(ya, yb, h, lru_out_norm[l].reshape(1, LRU_WIDTH),
                      attn_out_norm[l].reshape(1, ATTN_WIDTH), w_out_mix[l].astype(bf16))

        last = l == depth - 1
        h = _ffn(h, norm_ffn2[l].reshape(1, d), ffn2_w_in[l].astype(bf16), ffn2_w_out[l].astype(bf16),
                 final_gain, final_norm=last)
    return h.reshape(bsz, s, d)
```

```python
import functools
import math

import jax
import jax.numpy as jnp
from jax import lax
from jax.experimental import pallas as pl
from jax.experimental.pallas import tpu as pltpu

D_MODEL = 2048
HEAD_DIM = 64
LRU_WIDTH = 1024
ATTN_WIDTH = 1024
ATTN_HEADS = ATTN_WIDTH // HEAD_DIM
IN_PROJ_WIDTH = 2 * LRU_WIDTH + 3 * ATTN_WIDTH
CONV_WIDTH = 4
CONV_PAD_L = CONV_WIDTH // 2
RG_C = 8.0
GRID_W = 64
WIN_ROWS = 8
WIN_COLS = 16
D_FF = 5632
NORM_EPS = 1e-6

LANES = 128
SUBLANES = 8
MASK_VALUE = -1e30

FFN_TM = 512
FFN_TF = 512
PROJ_TM = 1024
PROJ_TN = 1024
OUT_TM = 512
LRU_TB = 256
SCAN_UNROLL = 8
ATTN_SLOTS = 4
TIME_PAD = SUBLANES


def _vmem(mib):
    return pltpu.CompilerParams(
        dimension_semantics=("parallel", "arbitrary"), vmem_limit_bytes=mib * 1024 * 1024)


def _rms_norm(x, gain):
    ms = jnp.mean(x * x, axis=-1, keepdims=True)
    return x * lax.rsqrt(ms + NORM_EPS) * gain


def _ffn_kernel(x_ref, g_ref, wg_ref, wu_ref, wo_ref, gf_ref, o_ref, u_ref, *, final_norm):
    k = pl.program_id(1)

    @pl.when(k == 0)
    def _():
        x = x_ref[...]
        u_ref[...] = _rms_norm(x, g_ref[...]).astype(jnp.bfloat16)
        o_ref[...] = x

    u = u_ref[...]
    gate = jnp.dot(u, wg_ref[...], preferred_element_type=jnp.float32)
    up = jnp.dot(u, wu_ref[...], preferred_element_type=jnp.float32)
    act = (gate * jax.nn.sigmoid(gate)) * (0.5 * up)
    o_ref[...] += jnp.dot(act.astype(jnp.bfloat16), wo_ref[...],
                          preferred_element_type=jnp.float32)

    if final_norm:
        @pl.when(k == pl.num_programs(1) - 1)
        def _():
            o_ref[...] = _rms_norm(o_ref[...], gf_ref[...])


def _ffn(h, gain, w_in, w_out, final_gain, final_norm):
    t, d = h.shape
    nk = D_FF // FFN_TF
    return pl.pallas_call(
        functools.partial(_ffn_kernel, final_norm=final_norm),
        grid=(t // FFN_TM, nk),
        in_specs=[
            pl.BlockSpec((FFN_TM, d), lambda i, k: (i, 0)),
            pl.BlockSpec((1, d), lambda i, k: (0, 0)),
            pl.BlockSpec((d, FFN_TF), lambda i, k: (0, k)),
            pl.BlockSpec((d, FFN_TF), lambda i, k: (0, k + nk)),
            pl.BlockSpec((FFN_TF, d), lambda i, k: (k, 0)),
            pl.BlockSpec((1, d), lambda i, k: (0, 0)),
        ],
        out_specs=pl.BlockSpec((FFN_TM, d), lambda i, k: (i, 0)),
        out_shape=jax.ShapeDtypeStruct((t, d), jnp.float32),
        scratch_shapes=[pltpu.VMEM((FFN_TM, d), jnp.bfloat16)],
        compiler_params=_vmem(48),
        name="ffn_final" if final_norm else "ffn",
    )(h, gain, w_in, w_in, w_out, final_gain)


def _in_proj_kernel(x_ref, g_ref, w_ref, o_ref, u_ref):
    @pl.when(pl.program_id(1) == 0)
    def _():
        u_ref[...] = _rms_norm(x_ref[...], g_ref[...]).astype(jnp.bfloat16)

    o_ref[...] = jnp.dot(u_ref[...], w_ref[...], preferred_element_type=jnp.float32)


def _in_proj(h, gain, w):
    t, d = h.shape
    n = w.shape[1]
    return pl.pallas_call(
        _in_proj_kernel,
        grid=(t // PROJ_TM, n // PROJ_TN),
        in_specs=[
            pl.BlockSpec((PROJ_TM, d), lambda i, j: (i, 0)),
            pl.BlockSpec((1, d), lambda i, j: (0, 0)),
            pl.BlockSpec((d, PROJ_TN), lambda i, j: (0, j)),
        ],
        out_specs=pl.BlockSpec((PROJ_TM, PROJ_TN), lambda i, j: (i, j)),
        out_shape=jax.ShapeDtypeStruct((t, n), jnp.float32),
        scratch_shapes=[pltpu.VMEM((PROJ_TM, d), jnp.bfloat16)],
        compiler_params=_vmem(48),
        name="in_proj",
    )(h, gain, w)


def _lru_kernel(*refs, reverse, n_chunks):
    if reverse:
        x_ref, w_ref, gb_ref, cw_ref, cb_ref, lam_ref, hf_ref, g_ref, o_ref, a_scr, b_scr, h_scr = refs
    else:
        x_ref, w_ref, gb_ref, cw_ref, cb_ref, lam_ref, o_ref, a_scr, b_scr, h_scr = refs
    j = pl.program_id(1)
    chunk = (n_chunks - 1 - j) if reverse else j
    t0 = pl.multiple_of(chunk * LRU_TB, LRU_TB)

    @pl.when(j == 0)
    def _():
        h_scr[...] = jnp.zeros_like(h_scr)

    xc = cb_ref[...][None]
    for tap in range(CONV_WIDTH):
        xc = xc + cw_ref[tap:tap + 1, :][None] * x_ref[pl.ds(t0 + tap, LRU_TB)]
    rows = LRU_TB * SUBLANES
    xc2 = xc.reshape(rows, LANES)
    pre = jnp.dot(xc2.astype(jnp.bfloat16), w_ref[0], preferred_element_type=jnp.float32)
    pre = pre + gb_ref[0]
    r = jax.nn.sigmoid(pre[:, :LANES])
    i = jax.nn.sigmoid(pre[:, LANES:])
    z = -lam_ref[0]
    softplus = jnp.maximum(z, 0.0) + jnp.log1p(jnp.exp(-jnp.abs(z)))
    log_a = (-RG_C * softplus) * r
    a = jnp.exp(log_a)
    b = jnp.sqrt(-jnp.tanh(log_a) * (a * a + 1.0)) * (i * xc2)
    a_scr[...] = a.reshape(LRU_TB, SUBLANES, LANES)
    b_scr[...] = b.reshape(LRU_TB, SUBLANES, LANES)

    def step(s, h):
        t = (LRU_TB - 1 - s) if reverse else s
        h = a_scr[t] * h + b_scr[t]
        o_ref[t] = h
        return h

    h_scr[...] = lax.fori_loop(0, LRU_TB, step, h_scr[...], unroll=SCAN_UNROLL)

    if reverse:
        o_ref[...] = jax.nn.gelu(g_ref[...], approximate=True) * (hf_ref[...] + o_ref[...])


def _lru_direction(xs, gate_w, gate_b, conv_w, conv_b, lam, hf, gs, reverse):
    s_pad, bsz, c = xs.shape
    s = s_pad - TIME_PAD
    n_cblk = c // LANES
    n_chunks = s // LRU_TB
    chunk_of = (lambda j: n_chunks - 1 - j) if reverse else (lambda j: j)
    chunk_spec = pl.BlockSpec((LRU_TB, bsz, LANES), lambda cb, j: (chunk_of(j), 0, cb))
    in_specs = [
        pl.BlockSpec((s_pad, bsz, LANES), lambda cb, j: (0, 0, cb)),
        pl.BlockSpec((1, LANES, 2 * LANES), lambda cb, j: (cb, 0, 0)),
        pl.BlockSpec((1, 1, 2 * LANES), lambda cb, j: (cb, 0, 0)),
        pl.BlockSpec((CONV_WIDTH, LANES), lambda cb, j: (0, cb)),
        pl.BlockSpec((1, LANES), lambda cb, j: (0, cb)),
        pl.BlockSpec((1, 1, LANES), lambda cb, j: (0, 0, cb)),
    ]
    args = [xs, gate_w, gate_b, conv_w, conv_b, lam]
    if reverse:
        in_specs += [chunk_spec, chunk_spec]
        args += [hf, gs]
    return pl.pallas_call(
        functools.partial(_lru_kernel, reverse=reverse, n_chunks=n_chunks),
        grid=(n_cblk, n_chunks),
        in_specs=in_specs,
        out_specs=chunk_spec,
        out_shape=jax.ShapeDtypeStruct((s, bsz, c), jnp.float32),
        scratch_shapes=[
            pltpu.VMEM((LRU_TB, bsz, LANES), jnp.float32),
            pltpu.VMEM((LRU_TB, bsz, LANES), jnp.float32),
            pltpu.VMEM((bsz, LANES), jnp.float32),
        ],
        compiler_params=_vmem(48),
        name="lru_bwd" if reverse else "lru_fwd",
    )(*args)


def _lru_gate_weights(gate_w, gate_b):
    n_cblk = LRU_WIDTH // LANES
    hpb = LANES // HEAD_DIM
    w = gate_w.reshape(2, 2, n_cblk, hpb, HEAD_DIM, HEAD_DIM)
    eye = jnp.eye(hpb, dtype=gate_w.dtype)
    wbd = w[:, :, :, :, :, None, :] * eye[None, None, None, :, None, :, None]
    wbd = wbd.reshape(2, 2, n_cblk, LANES, LANES)
    wbd = wbd.transpose(0, 2, 3, 1, 4).reshape(2, n_cblk, LANES, 2 * LANES)
    bias = gate_b.reshape(2, 2, n_cblk, LANES).transpose(0, 2, 1, 3).reshape(2, n_cblk, 1, 2 * LANES)
    return wbd.astype(jnp.bfloat16), bias


def _attn_kernel(q_ref, k_ref, v_ref, bias_ref, o_ref, q_scr, k_scr, v_scr, s_scr, p_scr, *, rows, kr):
    lane = lax.broadcasted_iota(jnp.int32, (1, 1, LANES), 2)
    first_head = lane < HEAD_DIM
    q = (q_ref[0] * HEAD_DIM ** -0.5).reshape(rows, GRID_W, LANES)
    q_scr[:, :GRID_W, :] = jnp.where(first_head, q, 0.0).astype(jnp.bfloat16)
    q_scr[:, GRID_W:, :] = jnp.where(first_head, 0.0, q).astype(jnp.bfloat16)
    k_scr[...] = k_ref[0].astype(jnp.bfloat16)
    v_scr[:, :LANES] = v_ref[0].astype(jnp.bfloat16)
    v_scr[:, LANES:] = jnp.ones((rows * GRID_W, LANES), jnp.bfloat16)
    band = kr * GRID_W

    def band_start(r):
        return min(max(r - kr // 2, 0), rows - kr)

    def scores(r):
        rs = band_start(r)
        s_t = lax.dot_general(k_scr[rs * GRID_W:rs * GRID_W + band, :], q_scr[r], (((1,), (1,)), ((), ())),
                              preferred_element_type=jnp.float32)
        s_t = s_t + bias_ref[0, r - rs]
        s_scr[r % ATTN_SLOTS] = s_t
        return jnp.max(s_t, axis=0, keepdims=True)

    def exponentials(r, m):
        p_scr[r % ATTN_SLOTS] = jnp.exp(s_scr[r % ATTN_SLOTS] - m).astype(jnp.bfloat16)

    def weighted_values(r):
        rs = band_start(r)
        pv = lax.dot_general(p_scr[r % ATTN_SLOTS], v_scr[rs * GRID_W:rs * GRID_W + band, :],
                             (((0,), (0,)), ((), ())), preferred_element_type=jnp.float32)
        out = pv[:, :LANES] / pv[:, LANES:]
        o_ref[0, r * GRID_W:(r + 1) * GRID_W, :] = jnp.where(first_head[0], out[:GRID_W], out[GRID_W:])

    row_max = {}
    for step in range(rows + 2):
        if step >= 2:
            weighted_values(step - 2)
        if 1 <= step <= rows:
            exponentials(step - 1, row_max.pop(step - 1))
        if step < rows:
            row_max[step] = scores(step)


def _attention(proj, bias, bsz, s):
    rows = s // GRID_W
    kr = min(WIN_ROWS, rows)
    n_hp = ATTN_WIDTH // LANES
    q_blk = 2 * LRU_WIDTH // LANES
    k_blk = q_blk + n_hp
    v_blk = k_blk + n_hp
    blk = (1, s, LANES)
    return pl.pallas_call(
        functools.partial(_attn_kernel, rows=rows, kr=kr),
        grid=(n_hp, bsz),
        in_specs=[
            pl.BlockSpec(blk, lambda hp, b: (b, 0, q_blk + hp)),
            pl.BlockSpec(blk, lambda hp, b: (b, 0, k_blk + hp)),
            pl.BlockSpec(blk, lambda hp, b: (b, 0, v_blk + hp)),
            pl.BlockSpec((1, kr, kr * GRID_W, LANES), lambda hp, b: (hp, 0, 0, 0)),
        ],
        out_specs=pl.BlockSpec(blk, lambda hp, b: (b, 0, hp)),
        out_shape=jax.ShapeDtypeStruct((bsz, s, ATTN_WIDTH), jnp.float32),
        scratch_shapes=[
            pltpu.VMEM((rows, 2 * GRID_W, LANES), jnp.bfloat16),
            pltpu.VMEM((s, LANES), jnp.bfloat16),
            pltpu.VMEM((s, 2 * LANES), jnp.bfloat16),
            pltpu.VMEM((ATTN_SLOTS, kr * GRID_W, LANES), jnp.float32),
            pltpu.VMEM((ATTN_SLOTS, kr * GRID_W, LANES), jnp.bfloat16),
        ],
        compiler_params=_vmem(32),
        name="attn",
    )(proj, proj, proj, bias)


def _attn_bias_slabs(rpb, rows):
    kr = min(WIN_ROWS, rows)
    n_heads = rpb.shape[0]
    cols = jnp.arange(GRID_W)
    col_start = jnp.clip(cols - WIN_COLS // 2, 0, GRID_W - WIN_COLS)
    rel = cols[:, None] - cols[None, :] + WIN_COLS - 1
    inside = (cols[:, None] >= col_start[None, :]) & (cols[:, None] < col_start[None, :] + WIN_COLS)
    per_row = jnp.where(inside[None, None], rpb[:, :, jnp.clip(rel, 0, 2 * WIN_COLS - 2)], MASK_VALUE)
    slabs = jnp.stack([per_row[:, WIN_ROWS - 1 - off:WIN_ROWS - 1 - off + kr] for off in range(kr)], axis=1)
    slabs = slabs.reshape(n_heads // 2, 2, kr, kr * GRID_W, GRID_W)
    return slabs.transpose(0, 2, 3, 1, 4).reshape(n_heads // 2, kr, kr * GRID_W, 2 * GRID_W)


def _out_proj_kernel(ya_ref, yb_ref, h_ref, ga_ref, gb_ref, wa_ref, wb_ref, o_ref):
    ua = _rms_norm(ya_ref[...], ga_ref[...]).astype(jnp.bfloat16)
    ub = _rms_norm(yb_ref[...], gb_ref[...]).astype(jnp.bfloat16)
    o_ref[...] = (h_ref[...]
                  + jnp.dot(ua, wa_ref[...], preferred_element_type=jnp.float32)
                  + jnp.dot(ub, wb_ref[...], preferred_element_type=jnp.float32))


def _out_proj(ya, yb, h, ga, gb, w):
    t, d = h.shape
    ca, cb = ya.shape[1], yb.shape[1]
    return pl.pallas_call(
        _out_proj_kernel,
        grid=(t // OUT_TM, 1),
        in_specs=[
            pl.BlockSpec((OUT_TM, ca), lambda i, j: (i, 0)),
            pl.BlockSpec((OUT_TM, cb), lambda i, j: (i, 0)),
            pl.BlockSpec((OUT_TM, d), lambda i, j: (i, 0)),
            pl.BlockSpec((1, ca), lambda i, j: (0, 0)),
            pl.BlockSpec((1, cb), lambda i, j: (0, 0)),
            pl.BlockSpec((ca, d), lambda i, j: (0, 0)),
            pl.BlockSpec((cb, d), lambda i, j: (1, 0)),
        ],
        out_specs=pl.BlockSpec((OUT_TM, d), lambda i, j: (i, 0)),
        out_shape=jax.ShapeDtypeStruct((t, d), jnp.float32),
        compiler_params=_vmem(48),
        name="out_proj",
    )(ya, yb, h, ga, gb, w, w)


def kernel(x, norm_ffn1, ffn1_w_in, ffn1_w_out, norm_mix, w_in_mix, lru_conv_w, lru_conv_b, lru_gate_w, lru_gate_b, lru_lambda, attn_rpb, lru_out_norm, attn_out_norm, w_out_mix, norm_ffn2, ffn2_w_in, ffn2_w_out, norm_final):
    bsz, s, d = x.shape
    depth = norm_ffn1.shape[0]
    assert depth >= 1 and d == D_MODEL
    bf16 = jnp.bfloat16
    t = bsz * s
    h = x.reshape(t, d)
    final_gain = norm_final.reshape(1, d)
    for l in range(depth):
        h = _ffn(h, norm_ffn1[l].reshape(1, d), ffn1_w_in[l].astype(bf16), ffn1_w_out[l].astype(bf16),
                 final_gain, final_norm=False)

        proj = _in_proj(h, norm_mix[l].reshape(1, d), w_in_mix[l].astype(bf16))
        proj = proj.reshape(bsz, s, IN_PROJ_WIDTH)

        xs = jnp.transpose(proj[:, :, :LRU_WIDTH], (1, 0, 2))
        xs = jnp.pad(xs, ((CONV_PAD_L, TIME_PAD - CONV_PAD_L), (0, 0), (0, 0)))
        gs = jnp.transpose(proj[:, :, LRU_WIDTH:2 * LRU_WIDTH], (1, 0, 2))
        gate_w, gate_b = _lru_gate_weights(lru_gate_w[l], lru_gate_b[l])
        lam = lru_lambda[l].reshape(2, 1, LRU_WIDTH)
        conv_b = lru_conv_b[l].reshape(1, LRU_WIDTH)
        hf = _lru_direction(xs, gate_w[0], gate_b[0], lru_conv_w[l], conv_b, lam[0:1], None, None, False)
        ya = _lru_direction(xs, gate_w[1], gate_b[1], lru_conv_w[l], conv_b, lam[1:2], hf, gs, True)
        ya = jnp.transpose(ya, (1, 0, 2)).reshape(t, LRU_WIDTH)

        yb = _attention(proj, _attn_bias_slabs(attn_rpb[l], s // GRID_W), bsz, s).reshape(t, ATTN_WIDTH)

        h = _out_proj(ya, yb, h, lru_out_norm[l].reshape(1, LRU_WIDTH),
                      attn_out_norm[l].reshape(1, ATTN_WIDTH), w_out_mix[l].astype(bf16))

        last = l == depth - 1
        h = _ffn(h, norm_ffn2[l].reshape(1, d), ffn2_w_in[l].astype(bf16), ffn2_w_out[l].astype(bf16),
                 final_gain, final_norm=last)
    return h.reshape(bsz, s, d)
```

```python
import functools

import jax
import jax.numpy as jnp
import numpy as np
from jax import lax
from jax.experimental import pallas as pl
from jax.experimental.pallas import tpu as pltpu

D_MODEL = 2048
HEAD_DIM = 64
LRU_WIDTH = 1024
ATTN_WIDTH = 1024
ATTN_HEADS = ATTN_WIDTH // HEAD_DIM
IN_PROJ_WIDTH = 2 * LRU_WIDTH + 3 * ATTN_WIDTH
CONV_WIDTH = 4
CONV_PAD_L = CONV_WIDTH // 2
RG_C = 8.0
GRID_W = 64
WIN_ROWS = 8
WIN_COLS = 16
D_FF = 5632
NORM_EPS = 1e-6

LANES = 128
SUBLANES = 8
MASK_VALUE = -1e30

FFN_TM = 512
FFN_TF = 512
PROJ_TM = 1024
PROJ_TN = 1024
OUT_TM = 512
LRU_TB = 256
SCAN_UNROLL = 8
ATTN_SLOTS = 4


def _vmem(mib):
    return pltpu.CompilerParams(
        dimension_semantics=("parallel", "arbitrary"), vmem_limit_bytes=mib * 1024 * 1024)


def _rms_norm(x, gain):
    ms = jnp.mean(x * x, axis=-1, keepdims=True)
    return x * lax.rsqrt(ms + NORM_EPS) * gain


def _ffn_kernel(x_ref, g_ref, wg_ref, wu_ref, wo_ref, gf_ref, o_ref, u_ref, *, final_norm):
    k = pl.program_id(1)

    @pl.when(k == 0)
    def _():
        x = x_ref[...]
        u_ref[...] = _rms_norm(x, g_ref[...]).astype(jnp.bfloat16)
        o_ref[...] = x

    u = u_ref[...]
    gate = jnp.dot(u, wg_ref[...], preferred_element_type=jnp.float32)
    up = jnp.dot(u, wu_ref[...], preferred_element_type=jnp.float32)
    act = (gate * jax.nn.sigmoid(gate)) * (0.5 * up)
    o_ref[...] += jnp.dot(act.astype(jnp.bfloat16), wo_ref[...],
                          preferred_element_type=jnp.float32)

    if final_norm:
        @pl.when(k == pl.num_programs(1) - 1)
        def _():
            o_ref[...] = _rms_norm(o_ref[...], gf_ref[...])


def _ffn(h, gain, w_in, w_out, final_gain, final_norm):
    t, d = h.shape
    nk = D_FF // FFN_TF
    return pl.pallas_call(
        functools.partial(_ffn_kernel, final_norm=final_norm),
        grid=(t // FFN_TM, nk),
        in_specs=[
            pl.BlockSpec((FFN_TM, d), lambda i, k: (i, 0)),
            pl.BlockSpec((1, d), lambda i, k: (0, 0)),
            pl.BlockSpec((d, FFN_TF), lambda i, k: (0, k)),
            pl.BlockSpec((d, FFN_TF), lambda i, k: (0, k + nk)),
            pl.BlockSpec((FFN_TF, d), lambda i, k: (k, 0)),
            pl.BlockSpec((1, d), lambda i, k: (0, 0)),
        ],
        out_specs=pl.BlockSpec((FFN_TM, d), lambda i, k: (i, 0)),
        out_shape=jax.ShapeDtypeStruct((t, d), jnp.float32),
        scratch_shapes=[pltpu.VMEM((FFN_TM, d), jnp.bfloat16)],
        compiler_params=_vmem(48),
        name="ffn_final" if final_norm else "ffn",
    )(h, gain, w_in, w_in, w_out, final_gain)


def _in_proj_kernel(x_ref, g_ref, w_ref, lru_ref, qkv_ref, u_ref, *, n_lru):
    n = pl.program_id(1)
    bsz, ts, d = x_ref.shape

    @pl.when(n == 0)
    def _():
        u_ref[...] = _rms_norm(x_ref[...].reshape(bsz * ts, d), g_ref[...]).astype(jnp.bfloat16)

    res = jnp.dot(u_ref[...], w_ref[...], preferred_element_type=jnp.float32).reshape(bsz, ts, -1)

    @pl.when(n < n_lru)
    def _():
        lru_ref[...] = jnp.swapaxes(res, 0, 1)

    @pl.when(n >= n_lru)
    def _():
        qkv_ref[...] = res


def _in_proj(h, gain, w):
    bsz, s, d = h.shape
    ts = PROJ_TM // bsz
    n_lru = 2 * LRU_WIDTH // PROJ_TN
    n_qkv = 3 * ATTN_WIDTH // PROJ_TN
    return pl.pallas_call(
        functools.partial(_in_proj_kernel, n_lru=n_lru),
        grid=(s // ts, n_lru + n_qkv),
        in_specs=[
            pl.BlockSpec((bsz, ts, d), lambda i, n: (0, i, 0)),
            pl.BlockSpec((1, d), lambda i, n: (0, 0)),
            pl.BlockSpec((d, PROJ_TN), lambda i, n: (0, n)),
        ],
        out_specs=[
            pl.BlockSpec((ts, bsz, PROJ_TN), lambda i, n: (i, 0, jnp.minimum(n, n_lru - 1))),
            pl.BlockSpec((bsz, ts, PROJ_TN), lambda i, n: (0, i, jnp.maximum(n - n_lru, 0))),
        ],
        out_shape=[
            jax.ShapeDtypeStruct((s, bsz, 2 * LRU_WIDTH), jnp.float32),
            jax.ShapeDtypeStruct((bsz, s, 3 * ATTN_WIDTH), jnp.float32),
        ],
        scratch_shapes=[pltpu.VMEM((PROJ_TM, d), jnp.bfloat16)],
        compiler_params=_vmem(56),
        name="in_proj",
    )(h, gain, w)


def _lru_kernel(x_ref, gf_ref, gr_ref, w_ref, gb_ref, cw_ref, cb_ref, lam_ref, o_ref,
                xw_scr, a_scr, b_scr, hs_scr, h_scr, *, n_chunks):
    j = pl.program_id(1)
    seq = x_ref.shape[0]
    rows = LRU_TB * SUBLANES
    n_tail = CONV_WIDTH - 1 - CONV_PAD_L
    second_visit = j >= n_chunks // 2

    @pl.when(j == 0)
    def _():
        h_scr[...] = jnp.zeros_like(h_scr)

    starts = []
    for d in range(2):
        chunk = (n_chunks - 1 - j) if d else j
        t0 = pl.multiple_of(chunk * LRU_TB, LRU_TB)
        starts.append(t0)
        xw_scr[d, CONV_PAD_L:CONV_PAD_L + LRU_TB] = x_ref[pl.ds(t0, LRU_TB)]
        head = x_ref[pl.ds(jnp.maximum(t0 - CONV_PAD_L, 0), CONV_PAD_L)]
        xw_scr[d, :CONV_PAD_L] = jnp.where(chunk > 0, head, 0.0)
        tail = x_ref[pl.ds(jnp.minimum(t0 + LRU_TB, seq - n_tail), n_tail)]
        xw_scr[d, CONV_PAD_L + LRU_TB:CONV_PAD_L + LRU_TB + n_tail] = jnp.where(chunk < n_chunks - 1, tail, 0.0)

        xc = cb_ref[...][None]
        for tap in range(CONV_WIDTH):
            xc = xc + cw_ref[tap:tap + 1, :][None] * xw_scr[d, tap:tap + LRU_TB]
        xc2 = xc.reshape(rows, LANES)
        pre = jnp.dot(xc2.astype(jnp.bfloat16), w_ref[d, 0], preferred_element_type=jnp.float32)
        pre = pre + gb_ref[d, 0]
        r = jax.nn.sigmoid(pre[:, :LANES])
        i = jax.nn.sigmoid(pre[:, LANES:])
        z = -lam_ref[d]
        softplus = jnp.maximum(z, 0.0) + jnp.log1p(jnp.exp(-jnp.abs(z)))
        log_a = (-RG_C * softplus) * r
        a = jnp.exp(log_a)
        sq = -jnp.tanh(log_a) * (a * a + 1.0)
        b = jnp.where(sq > 0.0, sq * lax.rsqrt(sq), 0.0) * (i * xc2)
        a_scr[d] = a.reshape(LRU_TB, SUBLANES, LANES)
        b_scr[d] = b.reshape(LRU_TB, SUBLANES, LANES)

    def step(s, carry):
        hf, hr = carry
        tr = LRU_TB - 1 - s
        hf = a_scr[0, s] * hf + b_scr[0, s]
        hr = a_scr[1, tr] * hr + b_scr[1, tr]
        hs_scr[0, s] = hf
        hs_scr[1, tr] = hr
        return hf, hr

    hf, hr = lax.fori_loop(0, LRU_TB, step, (h_scr[0], h_scr[1]), unroll=SCAN_UNROLL)
    h_scr[0] = hf
    h_scr[1] = hr

    for d, g_ref in enumerate((gf_ref, gr_ref)):
        window = pl.ds(starts[d], LRU_TB)

        @pl.when(jnp.logical_not(second_visit))
        def _():
            o_ref[window] = hs_scr[d]

        @pl.when(second_visit)
        def _():
            o_ref[window] = jax.nn.gelu(g_ref[...], approximate=True) * (o_ref[window] + hs_scr[d])


def _lru(xg, gate_w, gate_b, conv_w, conv_b, lam):
    s, bsz, c2 = xg.shape
    c = c2 // 2
    n_cblk = c // LANES
    n_chunks = s // LRU_TB
    half = n_chunks // 2
    assert n_chunks % 2 == 0
    gate_fwd = pl.BlockSpec((LRU_TB, bsz, LANES), lambda cb, j: (jnp.maximum(j, half), 0, n_cblk + cb))
    gate_rev = pl.BlockSpec((LRU_TB, bsz, LANES),
                            lambda cb, j: (jnp.minimum(n_chunks - 1 - j, half - 1), 0, n_cblk + cb))
    return pl.pallas_call(
        functools.partial(_lru_kernel, n_chunks=n_chunks),
        grid=(n_cblk, n_chunks),
        in_specs=[
            pl.BlockSpec((s, bsz, LANES), lambda cb, j: (0, 0, cb)),
            gate_fwd,
            gate_rev,
            pl.BlockSpec((2, 1, LANES, 2 * LANES), lambda cb, j: (0, cb, 0, 0)),
            pl.BlockSpec((2, 1, 1, 2 * LANES), lambda cb, j: (0, cb, 0, 0)),
            pl.BlockSpec((CONV_WIDTH, LANES), lambda cb, j: (0, cb)),
            pl.BlockSpec((1, LANES), lambda cb, j: (0, cb)),
            pl.BlockSpec((2, 1, LANES), lambda cb, j: (0, 0, cb)),
        ],
        out_specs=pl.BlockSpec((s, bsz, LANES), lambda cb, j: (0, 0, cb)),
        out_shape=jax.ShapeDtypeStruct((s, bsz, c), jnp.float32),
        scratch_shapes=[
            pltpu.VMEM((2, LRU_TB + SUBLANES, bsz, LANES), jnp.float32),
            pltpu.VMEM((2, LRU_TB, bsz, LANES), jnp.float32),
            pltpu.VMEM((2, LRU_TB, bsz, LANES), jnp.float32),
            pltpu.VMEM((2, LRU_TB, bsz, LANES), jnp.float32),
            pltpu.VMEM((2, bsz, LANES), jnp.float32),
        ],
        compiler_params=_vmem(56),
        name="lru",
    )(xg, xg, xg, gate_w, gate_b, conv_w, conv_b, lam)


def _lru_gate_weights(gate_w, gate_b):
    n_cblk = LRU_WIDTH // LANES
    hpb = LANES // HEAD_DIM
    w = gate_w.reshape(2, 2, n_cblk, hpb, HEAD_DIM, HEAD_DIM)
    eye = jnp.eye(hpb, dtype=gate_w.dtype)
    wbd = w[:, :, :, :, :, None, :] * eye[None, None, None, :, None, :, None]
    wbd = wbd.reshape(2, 2, n_cblk, LANES, LANES)
    wbd = wbd.transpose(0, 2, 3, 1, 4).reshape(2, n_cblk, LANES, 2 * LANES)
    bias = gate_b.reshape(2, 2, n_cblk, LANES).transpose(0, 2, 1, 3).reshape(2, n_cblk, 1, 2 * LANES)
    return wbd.astype(jnp.bfloat16), bias


def _attn_kernel(q_ref, k_ref, v_ref, bias_ref, o_ref, q_scr, k_scr, v_scr, s_scr, p_scr, *, rows, kr):
    lane = lax.broadcasted_iota(jnp.int32, (1, 1, LANES), 2)
    first_head = lane < HEAD_DIM
    q = (q_ref[0] * HEAD_DIM ** -0.5).reshape(rows, GRID_W, LANES)
    q_scr[:, :GRID_W, :] = jnp.where(first_head, q, 0.0).astype(jnp.bfloat16)
    q_scr[:, GRID_W:, :] = jnp.where(first_head, 0.0, q).astype(jnp.bfloat16)
    k_scr[...] = k_ref[0].astype(jnp.bfloat16)
    v_scr[:, :LANES] = v_ref[0].astype(jnp.bfloat16)
    v_scr[:, LANES:] = jnp.ones((rows * GRID_W, LANES), jnp.bfloat16)
    band = kr * GRID_W

    def band_start(r):
        return min(max(r - kr // 2, 0), rows - kr)

    def scores(r):
        rs = band_start(r)
        s_t = lax.dot_general(k_scr[rs * GRID_W:rs * GRID_W + band, :], q_scr[r], (((1,), (1,)), ((), ())),
                              preferred_element_type=jnp.float32)
        s_t = s_t + bias_ref[0, r - rs]
        s_scr[r % ATTN_SLOTS] = s_t
        return jnp.max(s_t, axis=0, keepdims=True)

    def exponentials(r, m):
        p_scr[r % ATTN_SLOTS] = jnp.exp(s_scr[r % ATTN_SLOTS] - m).astype(jnp.bfloat16)

    def weighted_values(r):
        rs = band_start(r)
        pv = lax.dot_general(p_scr[r % ATTN_SLOTS], v_scr[rs * GRID_W:rs * GRID_W + band, :],
                             (((0,), (0,)), ((), ())), preferred_element_type=jnp.float32)
        out = pv[:, :LANES] / pv[:, LANES:]
        o_ref[0, r * GRID_W:(r + 1) * GRID_W, :] = jnp.where(first_head[0], out[:GRID_W], out[GRID_W:])

    row_max = {}
    for step in range(rows + 2):
        if step >= 2:
            weighted_values(step - 2)
        if 1 <= step <= rows:
            exponentials(step - 1, row_max.pop(step - 1))
        if step < rows:
            row_max[step] = scores(step)


def _attention(proj, bias):
    bsz, s, _ = proj.shape
    rows = s // GRID_W
    kr = min(WIN_ROWS, rows)
    n_hp = ATTN_WIDTH // LANES
    q_blk = 0
    k_blk = q_blk + n_hp
    v_blk = k_blk + n_hp
    blk = (1, s, LANES)
    return pl.pallas_call(
        functools.partial(_attn_kernel, rows=rows, kr=kr),
        grid=(n_hp, bsz),
        in_specs=[
            pl.BlockSpec(blk, lambda hp, b: (b, 0, q_blk + hp)),
            pl.BlockSpec(blk, lambda hp, b: (b, 0, k_blk + hp)),
            pl.BlockSpec(blk, lambda hp, b: (b, 0, v_blk + hp)),
            pl.BlockSpec((1, kr, kr * GRID_W, LANES), lambda hp, b: (hp, 0, 0, 0)),
        ],
        out_specs=pl.BlockSpec(blk, lambda hp, b: (b, 0, hp)),
        out_shape=jax.ShapeDtypeStruct((bsz, s, ATTN_WIDTH), jnp.float32),
        scratch_shapes=[
            pltpu.VMEM((rows, 2 * GRID_W, LANES), jnp.bfloat16),
            pltpu.VMEM((s, LANES), jnp.bfloat16),
            pltpu.VMEM((s, 2 * LANES), jnp.bfloat16),
            pltpu.VMEM((ATTN_SLOTS, kr * GRID_W, LANES), jnp.float32),
            pltpu.VMEM((ATTN_SLOTS, kr * GRID_W, LANES), jnp.bfloat16),
        ],
        compiler_params=_vmem(32),
        name="attn",
    )(proj, proj, proj, bias)


def _attn_bias_slabs(rpb, rows):
    kr = min(WIN_ROWS, rows)
    n_heads, n_row_off, n_rel = rpb.shape
    cols = np.arange(GRID_W)
    col_start = np.clip(cols - WIN_COLS // 2, 0, GRID_W - WIN_COLS)
    inside = (cols[None, :] >= col_start[:, None]) & (cols[None, :] < col_start[:, None] + WIN_COLS)
    period = 2 * GRID_W + 1
    lead = GRID_W - WIN_COLS
    ext = jnp.pad(rpb, ((0, 0), (0, 0), (lead, period - lead - n_rel)))
    flat = jnp.tile(ext, (1, 1, GRID_W + 1))[:, :, GRID_W - 1:GRID_W - 1 + 2 * GRID_W * GRID_W]
    toep = flat.reshape(n_heads, n_row_off, GRID_W, 2 * GRID_W)[..., :GRID_W]
    per_row = jnp.swapaxes(jnp.where(inside[None, None], toep, MASK_VALUE), 2, 3)
    slabs = jnp.stack([per_row[:, WIN_ROWS - 1 - off:WIN_ROWS - 1 - off + kr] for off in range(kr)], axis=1)
    slabs = slabs.reshape(n_heads // 2, 2, kr, kr * GRID_W, GRID_W)
    return slabs.transpose(0, 2, 3, 1, 4).reshape(n_heads // 2, kr, kr * GRID_W, 2 * GRID_W)


def _out_proj_kernel(ya_ref, yb_ref, h_ref, ga_ref, gb_ref, wa_ref, wb_ref, o_ref):
    bsz, ts, d = h_ref.shape
    ya = jnp.swapaxes(ya_ref[...], 0, 1).reshape(bsz * ts, -1)
    yb = yb_ref[...].reshape(bsz * ts, -1)
    ua = _rms_norm(ya, ga_ref[...]).astype(jnp.bfloat16)
    ub = _rms_norm(yb, gb_ref[...]).astype(jnp.bfloat16)
    out = (h_ref[...].reshape(bsz * ts, d)
           + jnp.dot(ua, wa_ref[...], preferred_element_type=jnp.float32)
           + jnp.dot(ub, wb_ref[...], preferred_element_type=jnp.float32))
    o_ref[...] = out.reshape(bsz, ts, d)


def _out_proj(ya, yb, h, ga, gb, w):
    bsz, s, d = h.shape
    ca, cb = ya.shape[2], yb.shape[2]
    ts = OUT_TM // bsz
    return pl.pallas_call(
        _out_proj_kernel,
        grid=(s // ts, 1),
        in_specs=[
            pl.BlockSpec((ts, bsz, ca), lambda i, j: (i, 0, 0)),
            pl.BlockSpec((bsz, ts, cb), lambda i, j: (0, i, 0)),
            pl.BlockSpec((bsz, ts, d), lambda i, j: (0, i, 0)),
            pl.BlockSpec((1, ca), lambda i, j: (0, 0)),
            pl.BlockSpec((1, cb), lambda i, j: (0, 0)),
            pl.BlockSpec((ca, d), lambda i, j: (0, 0)),
            pl.BlockSpec((cb, d), lambda i, j: (1, 0)),
        ],
        out_specs=pl.BlockSpec((bsz, ts, d), lambda i, j: (0, i, 0)),
        out_shape=jax.ShapeDtypeStruct((bsz, s, d), jnp.float32),
        compiler_params=_vmem(48),
        name="out_proj",
    )(ya, yb, h, ga, gb, w, w)


def kernel(x, norm_ffn1, ffn1_w_in, ffn1_w_out, norm_mix, w_in_mix, lru_conv_w, lru_conv_b, lru_gate_w, lru_gate_b, lru_lambda, attn_rpb, lru_out_norm, attn_out_norm, w_out_mix, norm_ffn2, ffn2_w_in, ffn2_w_out, norm_final):
    bsz, s, d = x.shape
    depth = norm_ffn1.shape[0]
    assert depth >= 1 and d == D_MODEL
    bf16 = jnp.bfloat16
    t = bsz * s
    h = x.reshape(t, d)
    final_gain = norm_final.reshape(1, d)
    for l in range(depth):
        h = _ffn(h, norm_ffn1[l].reshape(1, d), ffn1_w_in[l].astype(bf16), ffn1_w_out[l].astype(bf16),
                 final_gain, final_norm=False)

        h3 = h.reshape(bsz, s, d)
        xg, qkv = _in_proj(h3, norm_mix[l].reshape(1, d), w_in_mix[l].astype(bf16))

        gate_w, gate_b = _lru_gate_weights(lru_gate_w[l], lru_gate_b[l])
        ya = _lru(xg, gate_w, gate_b, lru_conv_w[l], lru_conv_b[l].reshape(1, LRU_WIDTH),
                  lru_lambda[l].reshape(2, 1, LRU_WIDTH))
        yb = _attention(qkv, _attn_bias_slabs(attn_rpb[l], s // GRID_W))

        h = _out_proj(ya, yb, h3, lru_out_norm[l].reshape(1, LRU_WIDTH),
                      attn_out_norm[l].reshape(1, ATTN_WIDTH), w_out_mix[l].astype(bf16)).reshape(t, d)

        last = l == depth - 1
        h = _ffn(h, norm_ffn2[l].reshape(1, d), ffn2_w_in[l].astype(bf16), ffn2_w_out[l].astype(bf16),
                 final_gain, final_norm=last)
    return h.reshape(bsz, s, d)
```

```python
import functools

import jax
import jax.numpy as jnp
import numpy as np
from jax import lax
from jax.experimental import pallas as pl
from jax.experimental.pallas import tpu as pltpu

D_MODEL = 2048
HEAD_DIM = 64
LRU_WIDTH = 1024
ATTN_WIDTH = 1024
ATTN_HEADS = ATTN_WIDTH // HEAD_DIM
IN_PROJ_WIDTH = 2 * LRU_WIDTH + 3 * ATTN_WIDTH
CONV_WIDTH = 4
CONV_PAD_L = CONV_WIDTH // 2
RG_C = 8.0
GRID_W = 64
WIN_ROWS = 8
WIN_COLS = 16
D_FF = 5632
NORM_EPS = 1e-6

LANES = 128
SUBLANES = 8
MASK_VALUE = -1e30

FFN_TM = 1024
FFN_TF = 512
PROJ_TM = 1024
PROJ_TN = 1024
OUT_TM = 512
LRU_TB = 256
SCAN_UNROLL = 8
ATTN_SLOTS = 4


def _vmem(mib):
    return pltpu.CompilerParams(
        dimension_semantics=("parallel", "arbitrary"), vmem_limit_bytes=mib * 1024 * 1024)


def _rms_norm(x, gain):
    ms = jnp.mean(x * x, axis=-1, keepdims=True)
    return x * lax.rsqrt(ms + NORM_EPS) * gain


def _ffn_kernel(x_ref, g_ref, wg_ref, wu_ref, wo_ref, gf_ref, o_ref, u_ref, *, final_norm):
    k = pl.program_id(1)

    @pl.when(k == 0)
    def _():
        x = x_ref[...]
        u_ref[...] = _rms_norm(x, g_ref[...]).astype(jnp.bfloat16)
        o_ref[...] = x

    u = u_ref[...]
    gate = jnp.dot(u, wg_ref[...], preferred_element_type=jnp.float32)
    up = jnp.dot(u, wu_ref[...], preferred_element_type=jnp.float32)
    act = (gate * jax.nn.sigmoid(gate)) * (0.5 * up)
    o_ref[...] += jnp.dot(act.astype(jnp.bfloat16), wo_ref[...],
                          preferred_element_type=jnp.float32)

    if final_norm:
        @pl.when(k == pl.num_programs(1) - 1)
        def _():
            o_ref[...] = _rms_norm(o_ref[...], gf_ref[...])


def _ffn(h, gain, w_in, w_out, final_gain, final_norm):
    t, d = h.shape
    nk = D_FF // FFN_TF
    return pl.pallas_call(
        functools.partial(_ffn_kernel, final_norm=final_norm),
        grid=(t // FFN_TM, nk),
        in_specs=[
            pl.BlockSpec((FFN_TM, d), lambda i, k: (i, 0)),
            pl.BlockSpec((1, d), lambda i, k: (0, 0)),
            pl.BlockSpec((d, FFN_TF), lambda i, k: (0, k)),
            pl.BlockSpec((d, FFN_TF), lambda i, k: (0, k + nk)),
            pl.BlockSpec((FFN_TF, d), lambda i, k: (k, 0)),
            pl.BlockSpec((1, d), lambda i, k: (0, 0)),
        ],
        out_specs=pl.BlockSpec((FFN_TM, d), lambda i, k: (i, 0)),
        out_shape=jax.ShapeDtypeStruct((t, d), jnp.float32),
        scratch_shapes=[pltpu.VMEM((FFN_TM, d), jnp.bfloat16)],
        compiler_params=_vmem(58),
        name="ffn_final" if final_norm else "ffn",
    )(h, gain, w_in, w_in, w_out, final_gain)


def _in_proj_kernel(x_ref, g_ref, w_ref, lru_ref, qkv_ref, u_ref, *, n_lru):
    n = pl.program_id(1)
    bsz, ts, d = x_ref.shape

    @pl.when(n == 0)
    def _():
        u_ref[...] = _rms_norm(x_ref[...].reshape(bsz * ts, d), g_ref[...]).astype(jnp.bfloat16)

    res = jnp.dot(u_ref[...], w_ref[...], preferred_element_type=jnp.float32).reshape(bsz, ts, -1)

    @pl.when(n < n_lru)
    def _():
        lru_ref[...] = jnp.swapaxes(res, 0, 1)

    @pl.when(n >= n_lru)
    def _():
        qkv_ref[...] = res.astype(qkv_ref.dtype)


def _in_proj(h, gain, w):
    bsz, s, d = h.shape
    ts = PROJ_TM // bsz
    n_lru = 2 * LRU_WIDTH // PROJ_TN
    n_qkv = 3 * ATTN_WIDTH // PROJ_TN
    return pl.pallas_call(
        functools.partial(_in_proj_kernel, n_lru=n_lru),
        grid=(s // ts, n_lru + n_qkv),
        in_specs=[
            pl.BlockSpec((bsz, ts, d), lambda i, n: (0, i, 0)),
            pl.BlockSpec((1, d), lambda i, n: (0, 0)),
            pl.BlockSpec((d, PROJ_TN), lambda i, n: (0, n)),
        ],
        out_specs=[
            pl.BlockSpec((ts, bsz, PROJ_TN), lambda i, n: (i, 0, jnp.minimum(n, n_lru - 1))),
            pl.BlockSpec((bsz, ts, PROJ_TN), lambda i, n: (0, i, jnp.maximum(n - n_lru, 0))),
        ],
        out_shape=[
            jax.ShapeDtypeStruct((s, bsz, 2 * LRU_WIDTH), jnp.float32),
            jax.ShapeDtypeStruct((bsz, s, 3 * ATTN_WIDTH), jnp.bfloat16),
        ],
        scratch_shapes=[pltpu.VMEM((PROJ_TM, d), jnp.bfloat16)],
        compiler_params=_vmem(56),
        name="in_proj",
    )(h, gain, w)


def _lru_kernel(x_ref, gf_ref, gr_ref, w_ref, gb_ref, cw_ref, cb_ref, lam_ref, o_ref,
                xw_scr, a_scr, b_scr, hs_scr, h_scr, *, n_chunks):
    j = pl.program_id(1)
    seq = x_ref.shape[0]
    rows = LRU_TB * SUBLANES
    n_tail = CONV_WIDTH - 1 - CONV_PAD_L
    second_visit = j >= n_chunks // 2

    @pl.when(j == 0)
    def _():
        h_scr[...] = jnp.zeros_like(h_scr)

    starts = []
    for d in range(2):
        chunk = (n_chunks - 1 - j) if d else j
        t0 = pl.multiple_of(chunk * LRU_TB, LRU_TB)
        starts.append(t0)
        xw_scr[d, CONV_PAD_L:CONV_PAD_L + LRU_TB] = x_ref[pl.ds(t0, LRU_TB)]
        head = x_ref[pl.ds(jnp.maximum(t0 - CONV_PAD_L, 0), CONV_PAD_L)]
        xw_scr[d, :CONV_PAD_L] = jnp.where(chunk > 0, head, 0.0)
        tail = x_ref[pl.ds(jnp.minimum(t0 + LRU_TB, seq - n_tail), n_tail)]
        xw_scr[d, CONV_PAD_L + LRU_TB:CONV_PAD_L + LRU_TB + n_tail] = jnp.where(chunk < n_chunks - 1, tail, 0.0)

        xc = cb_ref[...][None]
        for tap in range(CONV_WIDTH):
            xc = xc + cw_ref[tap:tap + 1, :][None] * xw_scr[d, tap:tap + LRU_TB]
        xc2 = xc.reshape(rows, LANES)
        pre = jnp.dot(xc2.astype(jnp.bfloat16), w_ref[d, 0], preferred_element_type=jnp.float32)
        pre = pre + gb_ref[d, 0]
        r = jax.nn.sigmoid(pre[:, :LANES])
        i = jax.nn.sigmoid(pre[:, LANES:])
        z = -lam_ref[d]
        softplus = jnp.maximum(z, 0.0) + jnp.log1p(jnp.exp(-jnp.abs(z)))
        log_a = (-RG_C * softplus) * r
        a = jnp.exp(log_a)
        sq = -jnp.tanh(log_a) * (a * a + 1.0)
        b = jnp.where(sq > 0.0, sq * lax.rsqrt(sq), 0.0) * (i * xc2)
        a_scr[d] = a.reshape(LRU_TB, SUBLANES, LANES)
        b_scr[d] = b.reshape(LRU_TB, SUBLANES, LANES)

    def step(s, carry):
        hf, hr = carry
        tr = LRU_TB - 1 - s
        hf = a_scr[0, s] * hf + b_scr[0, s]
        hr = a_scr[1, tr] * hr + b_scr[1, tr]
        hs_scr[0, s] = hf
        hs_scr[1, tr] = hr
        return hf, hr

    hf, hr = lax.fori_loop(0, LRU_TB, step, (h_scr[0], h_scr[1]), unroll=SCAN_UNROLL)
    h_scr[0] = hf
    h_scr[1] = hr

    for d, g_ref in enumerate((gf_ref, gr_ref)):
        window = pl.ds(starts[d], LRU_TB)

        @pl.when(jnp.logical_not(second_visit))
        def _():
            o_ref[window] = hs_scr[d]

        @pl.when(second_visit)
        def _():
            o_ref[window] = jax.nn.gelu(g_ref[...], approximate=True) * (o_ref[window] + hs_scr[d])


def _lru(xg, gate_w, gate_b, conv_w, conv_b, lam):
    s, bsz, c2 = xg.shape
    c = c2 // 2
    n_cblk = c // LANES
    n_chunks = s // LRU_TB
    half = n_chunks // 2
    assert n_chunks % 2 == 0
    gate_fwd = pl.BlockSpec((LRU_TB, bsz, LANES), lambda cb, j: (jnp.maximum(j, half), 0, n_cblk + cb))
    gate_rev = pl.BlockSpec((LRU_TB, bsz, LANES),
                            lambda cb, j: (jnp.minimum(n_chunks - 1 - j, half - 1), 0, n_cblk + cb))
    return pl.pallas_call(
        functools.partial(_lru_kernel, n_chunks=n_chunks),
        grid=(n_cblk, n_chunks),
        in_specs=[
            pl.BlockSpec((s, bsz, LANES), lambda cb, j: (0, 0, cb)),
            gate_fwd,
            gate_rev,
            pl.BlockSpec((2, 1, LANES, 2 * LANES), lambda cb, j: (0, cb, 0, 0)),
            pl.BlockSpec((2, 1, 1, 2 * LANES), lambda cb, j: (0, cb, 0, 0)),
            pl.BlockSpec((CONV_WIDTH, LANES), lambda cb, j: (0, cb)),
            pl.BlockSpec((1, LANES), lambda cb, j: (0, cb)),
            pl.BlockSpec((2, 1, LANES), lambda cb, j: (0, 0, cb)),
        ],
        out_specs=pl.BlockSpec((s, bsz, LANES), lambda cb, j: (0, 0, cb)),
        out_shape=jax.ShapeDtypeStruct((s, bsz, c), jnp.float32),
        scratch_shapes=[
            pltpu.VMEM((2, LRU_TB + SUBLANES, bsz, LANES), jnp.float32),
            pltpu.VMEM((2, LRU_TB, bsz, LANES), jnp.float32),
            pltpu.VMEM((2, LRU_TB, bsz, LANES), jnp.float32),
            pltpu.VMEM((2, LRU_TB, bsz, LANES), jnp.float32),
            pltpu.VMEM((2, bsz, LANES), jnp.float32),
        ],
        compiler_params=_vmem(56),
        name="lru",
    )(xg, xg, xg, gate_w, gate_b, conv_w, conv_b, lam)


def _lru_gate_weights(gate_w, gate_b):
    n_cblk = LRU_WIDTH // LANES
    hpb = LANES // HEAD_DIM
    w = gate_w.reshape(2, 2, n_cblk, hpb, HEAD_DIM, HEAD_DIM)
    eye = jnp.eye(hpb, dtype=gate_w.dtype)
    wbd = w[:, :, :, :, :, None, :] * eye[None, None, None, :, None, :, None]
    wbd = wbd.reshape(2, 2, n_cblk, LANES, LANES)
    wbd = wbd.transpose(0, 2, 3, 1, 4).reshape(2, n_cblk, LANES, 2 * LANES)
    bias = gate_b.reshape(2, 2, n_cblk, LANES).transpose(0, 2, 1, 3).reshape(2, n_cblk, 1, 2 * LANES)
    return wbd.astype(jnp.bfloat16), bias


def _attn_kernel(q_ref, k_ref, v_ref, bias_ref, o_ref, q_scr, v_scr, s_scr, p_scr, *, rows, kr):
    lane = lax.broadcasted_iota(jnp.int32, (1, 1, LANES), 2)
    first_head = lane < HEAD_DIM
    q = q_ref[0].reshape(rows, GRID_W, LANES) * HEAD_DIM ** -0.5
    q_scr[:, :GRID_W, :] = jnp.where(first_head, q, 0.0)
    q_scr[:, GRID_W:, :] = jnp.where(first_head, 0.0, q)
    v_scr[:, :LANES] = v_ref[0]
    v_scr[:, LANES:] = jnp.ones((rows * GRID_W, LANES), jnp.bfloat16)
    band = kr * GRID_W

    def band_start(r):
        return min(max(r - kr // 2, 0), rows - kr)

    def scores(r):
        rs = band_start(r)
        s_t = lax.dot_general(k_ref[0, rs * GRID_W:rs * GRID_W + band, :], q_scr[r], (((1,), (1,)), ((), ())),
                              preferred_element_type=jnp.float32)
        s_t = s_t + jnp.concatenate([bias_ref[0, rs + i - r + WIN_ROWS - 1] for i in range(kr)], axis=0)
        s_scr[r % ATTN_SLOTS] = s_t
        return jnp.max(s_t, axis=0, keepdims=True)

    def exponentials(r, m):
        p_scr[r % ATTN_SLOTS] = jnp.exp(s_scr[r % ATTN_SLOTS] - m).astype(jnp.bfloat16)

    def weighted_values(r):
        rs = band_start(r)
        pv = lax.dot_general(p_scr[r % ATTN_SLOTS], v_scr[rs * GRID_W:rs * GRID_W + band, :],
                             (((0,), (0,)), ((), ())), preferred_element_type=jnp.float32)
        out = pv[:, :LANES] / pv[:, LANES:]
        o_ref[0, r * GRID_W:(r + 1) * GRID_W, :] = jnp.where(first_head[0], out[:GRID_W], out[GRID_W:])

    row_max = {}
    for step in range(rows + 2):
        if step >= 2:
            weighted_values(step - 2)
        if 1 <= step <= rows:
            exponentials(step - 1, row_max.pop(step - 1))
        if step < rows:
            row_max[step] = scores(step)


def _attention(proj, bias):
    bsz, s, _ = proj.shape
    rows = s // GRID_W
    kr = min(WIN_ROWS, rows)
    n_hp = ATTN_WIDTH // LANES
    q_blk = 0
    k_blk = q_blk + n_hp
    v_blk = k_blk + n_hp
    blk = (1, s, LANES)
    return pl.pallas_call(
        functools.partial(_attn_kernel, rows=rows, kr=kr),
        grid=(n_hp, bsz),
        in_specs=[
            pl.BlockSpec(blk, lambda hp, b: (b, 0, q_blk + hp)),
            pl.BlockSpec(blk, lambda hp, b: (b, 0, k_blk + hp)),
            pl.BlockSpec(blk, lambda hp, b: (b, 0, v_blk + hp)),
            pl.BlockSpec((1, 2 * WIN_ROWS - 1, GRID_W, LANES), lambda hp, b: (hp, 0, 0, 0)),
        ],
        out_specs=pl.BlockSpec(blk, lambda hp, b: (b, 0, hp)),
        out_shape=jax.ShapeDtypeStruct((bsz, s, ATTN_WIDTH), jnp.float32),
        scratch_shapes=[
            pltpu.VMEM((rows, 2 * GRID_W, LANES), jnp.bfloat16),
            pltpu.VMEM((s, 2 * LANES), jnp.bfloat16),
            pltpu.VMEM((ATTN_SLOTS, kr * GRID_W, LANES), jnp.float32),
            pltpu.VMEM((ATTN_SLOTS, kr * GRID_W, LANES), jnp.bfloat16),
        ],
        compiler_params=_vmem(32),
        name="attn",
    )(proj, proj, proj, bias)


def _attn_bias_table(rpb):
    n_heads, n_row_off, n_rel = rpb.shape
    cols = np.arange(GRID_W)
    col_start = np.clip(cols - WIN_COLS // 2, 0, GRID_W - WIN_COLS)
    inside = (cols[None, :] >= col_start[:, None]) & (cols[None, :] < col_start[:, None] + WIN_COLS)
    period = 2 * GRID_W + 1
    lead = GRID_W - WIN_COLS
    ext = jnp.pad(rpb, ((0, 0), (0, 0), (lead, period - lead - n_rel)))
    flat = jnp.tile(ext, (1, 1, GRID_W + 1))[:, :, GRID_W - 1:GRID_W - 1 + 2 * GRID_W * GRID_W]
    toep = flat.reshape(n_heads, n_row_off, GRID_W, 2 * GRID_W)[..., :GRID_W]
    per_row = jnp.swapaxes(jnp.where(inside[None, None], toep, MASK_VALUE), 2, 3)
    per_row = per_row.reshape(n_heads // 2, 2, n_row_off, GRID_W, GRID_W)
    return per_row.transpose(0, 2, 3, 1, 4).reshape(n_heads // 2, n_row_off, GRID_W, 2 * GRID_W)


def _out_proj_kernel(ya_ref, yb_ref, h_ref, ga_ref, gb_ref, wa_ref, wb_ref, o_ref):
    bsz, ts, d = h_ref.shape
    ya = jnp.swapaxes(ya_ref[...], 0, 1).reshape(bsz * ts, -1)
    yb = yb_ref[...].reshape(bsz * ts, -1)
    ua = _rms_norm(ya, ga_ref[...]).astype(jnp.bfloat16)
    ub = _rms_norm(yb, gb_ref[...]).astype(jnp.bfloat16)
    out = (h_ref[...].reshape(bsz * ts, d)
           + jnp.dot(ua, wa_ref[...], preferred_element_type=jnp.float32)
           + jnp.dot(ub, wb_ref[...], preferred_element_type=jnp.float32))
    o_ref[...] = out.reshape(bsz, ts, d)


def _out_proj(ya, yb, h, ga, gb, w):
    bsz, s, d = h.shape
    ca, cb = ya.shape[2], yb.shape[2]
    ts = OUT_TM // bsz
    return pl.pallas_call(
        _out_proj_kernel,
        grid=(s // ts, 1),
        in_specs=[
            pl.BlockSpec((ts, bsz, ca), lambda i, j: (i, 0, 0)),
            pl.BlockSpec((bsz, ts, cb), lambda i, j: (0, i, 0)),
            pl.BlockSpec((bsz, ts, d), lambda i, j: (0, i, 0)),
            pl.BlockSpec((1, ca), lambda i, j: (0, 0)),
            pl.BlockSpec((1, cb), lambda i, j: (0, 0)),
            pl.BlockSpec((ca, d), lambda i, j: (0, 0)),
            pl.BlockSpec((cb, d), lambda i, j: (1, 0)),
        ],
        out_specs=pl.BlockSpec((bsz, ts, d), lambda i, j: (0, i, 0)),
        out_shape=jax.ShapeDtypeStruct((bsz, s, d), jnp.float32),
        compiler_params=_vmem(48),
        name="out_proj",
    )(ya, yb, h, ga, gb, w, w)


def kernel(x, norm_ffn1, ffn1_w_in, ffn1_w_out, norm_mix, w_in_mix, lru_conv_w, lru_conv_b, lru_gate_w, lru_gate_b, lru_lambda, attn_rpb, lru_out_norm, attn_out_norm, w_out_mix, norm_ffn2, ffn2_w_in, ffn2_w_out, norm_final):
    bsz, s, d = x.shape
    depth = norm_ffn1.shape[0]
    assert depth >= 1 and d == D_MODEL
    bf16 = jnp.bfloat16
    t = bsz * s
    h = x.reshape(t, d)
    final_gain = norm_final.reshape(1, d)
    for l in range(depth):
        h = _ffn(h, norm_ffn1[l].reshape(1, d), ffn1_w_in[l].astype(bf16), ffn1_w_out[l].astype(bf16),
                 final_gain, final_norm=False)

        h3 = h.reshape(bsz, s, d)
        xg, qkv = _in_proj(h3, norm_mix[l].reshape(1, d), w_in_mix[l].astype(bf16))

        gate_w, gate_b = _lru_gate_weights(lru_gate_w[l], lru_gate_b[l])
        ya = _lru(xg, gate_w, gate_b, lru_conv_w[l], lru_conv_b[l].reshape(1, LRU_WIDTH),
                  lru_lambda[l].reshape(2, 1, LRU_WIDTH))
        yb = _attention(qkv, _attn_bias_table(attn_rpb[l]))

        h = _out_proj(ya, yb, h3, lru_out_norm[l].reshape(1, LRU_WIDTH),
                      attn_out_norm[l].reshape(1, ATTN_WIDTH), w_out_mix[l].astype(bf16)).reshape(t, d)

        last = l == depth - 1
        h = _ffn(h, norm_ffn2[l].reshape(1, d), ffn2_w_in[l].astype(bf16), ffn2_w_out[l].astype(bf16),
                 final_gain, final_norm=last)
    return h.reshape(bsz, s, d)
```

```python
import functools

import jax
import jax.numpy as jnp
import numpy as np
from jax import lax
from jax.experimental import pallas as pl
from jax.experimental.pallas import tpu as pltpu

D_MODEL = 2048
HEAD_DIM = 64
LRU_WIDTH = 1024
ATTN_WIDTH = 1024
ATTN_HEADS = ATTN_WIDTH // HEAD_DIM
IN_PROJ_WIDTH = 2 * LRU_WIDTH + 3 * ATTN_WIDTH
CONV_WIDTH = 4
CONV_PAD_L = CONV_WIDTH // 2
RG_C = 8.0
GRID_W = 64
WIN_ROWS = 8
WIN_COLS = 16
D_FF = 5632
NORM_EPS = 1e-6

LANES = 128
SUBLANES = 8
BF16_SUBLANES = 16
MASK_VALUE = -1e30

FFN_TM = 1024
FFN_TF = 512
PROJ_TM = 1024
PROJ_TN = 1024
OUT_TM = 512
LRU_TB = 256
SCAN_UNROLL = 8
ATTN_SLOTS = 4


def _vmem(mib):
    return pltpu.CompilerParams(
        dimension_semantics=("parallel", "arbitrary"), vmem_limit_bytes=mib * 1024 * 1024)


def _rms_norm(x, gain):
    ms = jnp.mean(x * x, axis=-1, keepdims=True)
    return x * lax.rsqrt(ms + NORM_EPS) * gain


def _ffn_kernel(*refs, final_norm, n_cast):
    x_ref, g_ref, wg_ref, wu_ref, wo_ref, gf_ref = refs[:6]
    cast_src = refs[6:6 + n_cast]
    o_ref = refs[6 + n_cast]
    cast_dst = refs[7 + n_cast:7 + 2 * n_cast]
    u_ref = refs[7 + 2 * n_cast]
    k = pl.program_id(1)

    @pl.when(k == 0)
    def _():
        x = x_ref[...]
        u_ref[...] = _rms_norm(x, g_ref[...]).astype(jnp.bfloat16)
        o_ref[...] = x

    _cast_blocks(cast_src, cast_dst)

    u = u_ref[...]
    gate = jnp.dot(u, wg_ref[...], preferred_element_type=jnp.float32)
    up = jnp.dot(u, wu_ref[...], preferred_element_type=jnp.float32)
    act = (gate * jax.nn.sigmoid(gate)) * (0.5 * up)
    o_ref[...] += jnp.dot(act.astype(jnp.bfloat16), wo_ref[...],
                          preferred_element_type=jnp.float32)

    if final_norm:
        @pl.when(k == pl.num_programs(1) - 1)
        def _():
            o_ref[...] = _rms_norm(o_ref[...], gf_ref[...])


def _cast_specs(arrays, n_steps, flat_step):
    specs, shapes = [], []
    for w in arrays:
        rows = BF16_SUBLANES
        while w.shape[0] % rows or w.shape[0] // rows > n_steps:
            rows += BF16_SUBLANES
        specs.append(pl.BlockSpec(
            (rows, w.shape[1]), lambda *idx, nb=w.shape[0] // rows: (jnp.minimum(flat_step(*idx), nb - 1), 0)))
        shapes.append(jax.ShapeDtypeStruct(w.shape, jnp.bfloat16))
    return specs, shapes


def _cast_blocks(srcs, dsts):
    for src, dst in zip(srcs, dsts):
        dst[...] = src[...].astype(dst.dtype)


def _ffn(h, gain, w_in, w_out, final_gain, final_norm, cast=()):
    t, d = h.shape
    nk = D_FF // FFN_TF
    cast_specs, cast_shapes = _cast_specs(cast, (t // FFN_TM) * nk, lambda i, k: i * nk + k)
    return pl.pallas_call(
        functools.partial(_ffn_kernel, final_norm=final_norm, n_cast=len(cast)),
        grid=(t // FFN_TM, nk),
        in_specs=[
            pl.BlockSpec((FFN_TM, d), lambda i, k: (i, 0)),
            pl.BlockSpec((1, d), lambda i, k: (0, 0)),
            pl.BlockSpec((d, FFN_TF), lambda i, k: (0, k)),
            pl.BlockSpec((d, FFN_TF), lambda i, k: (0, k + nk)),
            pl.BlockSpec((FFN_TF, d), lambda i, k: (k, 0)),
            pl.BlockSpec((1, d), lambda i, k: (0, 0)),
        ] + cast_specs,
        out_specs=[pl.BlockSpec((FFN_TM, d), lambda i, k: (i, 0))] + cast_specs,
        out_shape=[jax.ShapeDtypeStruct((t, d), jnp.float32)] + cast_shapes,
        scratch_shapes=[pltpu.VMEM((FFN_TM, d), jnp.bfloat16)],
        compiler_params=_vmem(58),
        name="ffn_final" if final_norm else "ffn",
    )(h, gain, w_in, w_in, w_out, final_gain, *cast)


def _in_proj_kernel(x_ref, g_ref, w_ref, lru_ref, qkv_ref, u_ref, *, n_lru):
    n = pl.program_id(1)
    bsz, ts, d = x_ref.shape

    @pl.when(n == 0)
    def _():
        u_ref[...] = _rms_norm(x_ref[...].reshape(bsz * ts, d), g_ref[...]).astype(jnp.bfloat16)

    res = jnp.dot(u_ref[...], w_ref[...], preferred_element_type=jnp.float32).reshape(bsz, ts, -1)

    @pl.when(n < n_lru)
    def _():
        lru_ref[...] = jnp.swapaxes(res, 0, 1)

    @pl.when(n >= n_lru)
    def _():
        qkv_ref[...] = res.astype(qkv_ref.dtype)


def _in_proj(h, gain, w):
    bsz, s, d = h.shape
    ts = PROJ_TM // bsz
    n_lru = 2 * LRU_WIDTH // PROJ_TN
    n_qkv = 3 * ATTN_WIDTH // PROJ_TN
    return pl.pallas_call(
        functools.partial(_in_proj_kernel, n_lru=n_lru),
        grid=(s // ts, n_lru + n_qkv),
        in_specs=[
            pl.BlockSpec((bsz, ts, d), lambda i, n: (0, i, 0)),
            pl.BlockSpec((1, d), lambda i, n: (0, 0)),
            pl.BlockSpec((d, PROJ_TN), lambda i, n: (0, n)),
        ],
        out_specs=[
            pl.BlockSpec((ts, bsz, PROJ_TN), lambda i, n: (i, 0, jnp.minimum(n, n_lru - 1))),
            pl.BlockSpec((bsz, ts, PROJ_TN), lambda i, n: (0, i, jnp.maximum(n - n_lru, 0))),
        ],
        out_shape=[
            jax.ShapeDtypeStruct((s, bsz, 2 * LRU_WIDTH), jnp.float32),
            jax.ShapeDtypeStruct((bsz, s, 3 * ATTN_WIDTH), jnp.bfloat16),
        ],
        scratch_shapes=[pltpu.VMEM((PROJ_TM, d), jnp.bfloat16)],
        compiler_params=_vmem(56),
        name="in_proj",
    )(h, gain, w)


def _lru_kernel(x_ref, gf_ref, gr_ref, w_ref, gb_ref, cw_ref, cb_ref, lam_ref, o_ref,
                xw_scr, a_scr, b_scr, hs_scr, h_scr, *, n_chunks):
    j = pl.program_id(1)
    seq = x_ref.shape[0]
    rows = LRU_TB * SUBLANES
    n_tail = CONV_WIDTH - 1 - CONV_PAD_L
    second_visit = j >= n_chunks // 2

    @pl.when(j == 0)
    def _():
        h_scr[...] = jnp.zeros_like(h_scr)

    starts = []
    for d in range(2):
        chunk = (n_chunks - 1 - j) if d else j
        t0 = pl.multiple_of(chunk * LRU_TB, LRU_TB)
        starts.append(t0)
        xw_scr[d, CONV_PAD_L:CONV_PAD_L + LRU_TB] = x_ref[pl.ds(t0, LRU_TB)]
        head = x_ref[pl.ds(jnp.maximum(t0 - CONV_PAD_L, 0), CONV_PAD_L)]
        xw_scr[d, :CONV_PAD_L] = jnp.where(chunk > 0, head, 0.0)
        tail = x_ref[pl.ds(jnp.minimum(t0 + LRU_TB, seq - n_tail), n_tail)]
        xw_scr[d, CONV_PAD_L + LRU_TB:CONV_PAD_L + LRU_TB + n_tail] = jnp.where(chunk < n_chunks - 1, tail, 0.0)

        xc = cb_ref[...][None]
        for tap in range(CONV_WIDTH):
            xc = xc + cw_ref[tap:tap + 1, :][None] * xw_scr[d, tap:tap + LRU_TB]
        xc2 = xc.reshape(rows, LANES)
        pre = jnp.dot(xc2.astype(jnp.bfloat16), w_ref[d, 0], preferred_element_type=jnp.float32)
        pre = pre + gb_ref[d, 0]
        r = jax.nn.sigmoid(pre[:, :LANES])
        i = jax.nn.sigmoid(pre[:, LANES:])
        z = -lam_ref[d]
        softplus = jnp.maximum(z, 0.0) + jnp.log1p(jnp.exp(-jnp.abs(z)))
        log_a = (-RG_C * softplus) * r
        a = jnp.exp(log_a)
        sq = -jnp.tanh(log_a) * (a * a + 1.0)
        b = jnp.where(sq > 0.0, sq * lax.rsqrt(sq), 0.0) * (i * xc2)
        a_scr[d] = a.reshape(LRU_TB, SUBLANES, LANES)
        b_scr[d] = b.reshape(LRU_TB, SUBLANES, LANES)

    def step(s, carry):
        hf, hr = carry
        tr = LRU_TB - 1 - s
        hf = a_scr[0, s] * hf + b_scr[0, s]
        hr = a_scr[1, tr] * hr + b_scr[1, tr]
        hs_scr[0, s] = hf
        hs_scr[1, tr] = hr
        return hf, hr

    hf, hr = lax.fori_loop(0, LRU_TB, step, (h_scr[0], h_scr[1]), unroll=SCAN_UNROLL)
    h_scr[0] = hf
    h_scr[1] = hr

    for d, g_ref in enumerate((gf_ref, gr_ref)):
        window = pl.ds(starts[d], LRU_TB)

        @pl.when(jnp.logical_not(second_visit))
        def _():
            o_ref[window] = hs_scr[d]

        @pl.when(second_visit)
        def _():
            o_ref[window] = jax.nn.gelu(g_ref[...], approximate=True) * (o_ref[window] + hs_scr[d])


def _lru(xg, gate_w, gate_b, conv_w, conv_b, lam):
    s, bsz, c2 = xg.shape
    c = c2 // 2
    n_cblk = c // LANES
    n_chunks = s // LRU_TB
    half = n_chunks // 2
    assert n_chunks % 2 == 0
    gate_fwd = pl.BlockSpec((LRU_TB, bsz, LANES), lambda cb, j: (jnp.maximum(j, half), 0, n_cblk + cb))
    gate_rev = pl.BlockSpec((LRU_TB, bsz, LANES),
                            lambda cb, j: (jnp.minimum(n_chunks - 1 - j, half - 1), 0, n_cblk + cb))
    return pl.pallas_call(
        functools.partial(_lru_kernel, n_chunks=n_chunks),
        grid=(n_cblk, n_chunks),
        in_specs=[
            pl.BlockSpec((s, bsz, LANES), lambda cb, j: (0, 0, cb)),
            gate_fwd,
            gate_rev,
            pl.BlockSpec((2, 1, LANES, 2 * LANES), lambda cb, j: (0, cb, 0, 0)),
            pl.BlockSpec((2, 1, 1, 2 * LANES), lambda cb, j: (0, cb, 0, 0)),
            pl.BlockSpec((CONV_WIDTH, LANES), lambda cb, j: (0, cb)),
            pl.BlockSpec((1, LANES), lambda cb, j: (0, cb)),
            pl.BlockSpec((2, 1, LANES), lambda cb, j: (0, 0, cb)),
        ],
        out_specs=pl.BlockSpec((s, bsz, LANES), lambda cb, j: (0, 0, cb)),
        out_shape=jax.ShapeDtypeStruct((s, bsz, c), jnp.float32),
        scratch_shapes=[
            pltpu.VMEM((2, LRU_TB + SUBLANES, bsz, LANES), jnp.float32),
            pltpu.VMEM((2, LRU_TB, bsz, LANES), jnp.float32),
            pltpu.VMEM((2, LRU_TB, bsz, LANES), jnp.float32),
            pltpu.VMEM((2, LRU_TB, bsz, LANES), jnp.float32),
            pltpu.VMEM((2, bsz, LANES), jnp.float32),
        ],
        compiler_params=_vmem(56),
        name="lru",
    )(xg, xg, xg, gate_w, gate_b, conv_w, conv_b, lam)


def _lru_gate_weights(gate_w, gate_b):
    n_cblk = LRU_WIDTH // LANES
    hpb = LANES // HEAD_DIM
    w = gate_w.reshape(2, 2, n_cblk, hpb, HEAD_DIM, HEAD_DIM)
    eye = jnp.eye(hpb, dtype=gate_w.dtype)
    wbd = w[:, :, :, :, :, None, :] * eye[None, None, None, :, None, :, None]
    wbd = wbd.reshape(2, 2, n_cblk, LANES, LANES)
    wbd = wbd.transpose(0, 2, 3, 1, 4).reshape(2, n_cblk, LANES, 2 * LANES)
    bias = gate_b.reshape(2, 2, n_cblk, LANES).transpose(0, 2, 1, 3).reshape(2, n_cblk, 1, 2 * LANES)
    return wbd.astype(jnp.bfloat16), bias


def _attn_kernel(*refs, rows, kr, n_cast):
    q_ref, k_ref, v_ref, bias_ref = refs[:4]
    o_ref = refs[4 + n_cast]
    q_scr, v_scr, p_scr = refs[5 + 2 * n_cast:]
    _cast_blocks(refs[4:4 + n_cast], refs[5 + n_cast:5 + 2 * n_cast])

    lane = lax.broadcasted_iota(jnp.int32, (1, 1, LANES), 2)
    first_head = lane < HEAD_DIM
    q = q_ref[0].reshape(rows, GRID_W, LANES) * HEAD_DIM ** -0.5
    for r in range(rows):
        stacked = jnp.concatenate([jnp.where(first_head[0], q[r], 0.0), jnp.where(first_head[0], 0.0, q[r])], axis=0)
        q_scr[r] = stacked.T
    v_scr[:, :LANES] = v_ref[0]
    v_scr[:, LANES:] = jnp.ones((rows * GRID_W, LANES), jnp.bfloat16)
    band = kr * GRID_W

    def band_start(r):
        return min(max(r - kr // 2, 0), rows - kr)

    def scores(r):
        rs = band_start(r)
        s_t = jnp.dot(k_ref[0, rs * GRID_W:rs * GRID_W + band, :], q_scr[r],
                      preferred_element_type=jnp.float32)
        s_t = s_t + jnp.concatenate([bias_ref[0, rs + i - r + WIN_ROWS - 1] for i in range(kr)], axis=0)
        return s_t

    def exponentials(r, s_t):
        m = jnp.max(s_t, axis=0, keepdims=True)
        p_scr[r % ATTN_SLOTS] = jnp.exp(s_t - m).astype(jnp.bfloat16)

    def weighted_values(r):
        rs = band_start(r)
        pv = lax.dot_general(p_scr[r % ATTN_SLOTS], v_scr[rs * GRID_W:rs * GRID_W + band, :],
                             (((0,), (0,)), ((), ())), preferred_element_type=jnp.float32)
        out = pv[:, :LANES] / pv[:, LANES:]
        o_ref[0, r * GRID_W:(r + 1) * GRID_W, :] = jnp.where(first_head[0], out[:GRID_W], out[GRID_W:])

    for step in range(rows + 2):
        if step >= 2:
            weighted_values(step - 2)
        if 1 <= step <= rows:
            exponentials(step - 1, pending)
        if step < rows:
            pending = scores(step)


def _attention(proj, bias, cast=()):
    bsz, s, _ = proj.shape
    rows = s // GRID_W
    kr = min(WIN_ROWS, rows)
    n_hp = ATTN_WIDTH // LANES
    q_blk = 0
    k_blk = q_blk + n_hp
    v_blk = k_blk + n_hp
    blk = (1, s, LANES)
    cast_specs, cast_shapes = _cast_specs(cast, n_hp * bsz, lambda hp, b: hp * bsz + b)
    return pl.pallas_call(
        functools.partial(_attn_kernel, rows=rows, kr=kr, n_cast=len(cast)),
        grid=(n_hp, bsz),
        in_specs=[
            pl.BlockSpec(blk, lambda hp, b: (b, 0, q_blk + hp)),
            pl.BlockSpec(blk, lambda hp, b: (b, 0, k_blk + hp)),
            pl.BlockSpec(blk, lambda hp, b: (b, 0, v_blk + hp)),
            pl.BlockSpec((1, 2 * WIN_ROWS - 1, GRID_W, LANES), lambda hp, b: (hp, 0, 0, 0)),
        ] + cast_specs,
        out_specs=[pl.BlockSpec(blk, lambda hp, b: (b, 0, hp))] + cast_specs,
        out_shape=[jax.ShapeDtypeStruct((bsz, s, ATTN_WIDTH), jnp.float32)] + cast_shapes,
        scratch_shapes=[
            pltpu.VMEM((rows, 2 * GRID_W, LANES), jnp.bfloat16),
            pltpu.VMEM((s, 2 * LANES), jnp.bfloat16),
            pltpu.VMEM((ATTN_SLOTS, kr * GRID_W, LANES), jnp.bfloat16),
        ],
        compiler_params=_vmem(32),
        name="attn",
    )(proj, proj, proj, bias, *cast)


def _attn_bias_table(rpb):
    n_heads, n_row_off, n_rel = rpb.shape
    cols = np.arange(GRID_W)
    col_start = np.clip(cols - WIN_COLS // 2, 0, GRID_W - WIN_COLS)
    inside = (cols[None, :] >= col_start[:, None]) & (cols[None, :] < col_start[:, None] + WIN_COLS)
    period = 2 * GRID_W + 1
    lead = GRID_W - WIN_COLS
    ext = jnp.pad(rpb, ((0, 0), (0, 0), (lead, period - lead - n_rel)))
    flat = jnp.tile(ext, (1, 1, GRID_W + 1))[:, :, GRID_W - 1:GRID_W - 1 + 2 * GRID_W * GRID_W]
    toep = flat.reshape(n_heads, n_row_off, GRID_W, 2 * GRID_W)[..., :GRID_W]
    per_row = jnp.swapaxes(jnp.where(inside[None, None], toep, MASK_VALUE), 2, 3)
    per_row = per_row.reshape(n_heads // 2, 2, n_row_off, GRID_W, GRID_W)
    return per_row.transpose(0, 2, 3, 1, 4).reshape(n_heads // 2, n_row_off, GRID_W, 2 * GRID_W)


def _out_proj_kernel(ya_ref, yb_ref, h_ref, ga_ref, gb_ref, wa_ref, wb_ref, o_ref):
    bsz, ts, d = h_ref.shape
    ya = jnp.swapaxes(ya_ref[...], 0, 1).reshape(bsz * ts, -1)
    yb = yb_ref[...].reshape(bsz * ts, -1)
    ua = _rms_norm(ya, ga_ref[...]).astype(jnp.bfloat16)
    ub = _rms_norm(yb, gb_ref[...]).astype(jnp.bfloat16)
    out = (h_ref[...].reshape(bsz * ts, d)
           + jnp.dot(ua, wa_ref[...], preferred_element_type=jnp.float32)
           + jnp.dot(ub, wb_ref[...], preferred_element_type=jnp.float32))
    o_ref[...] = out.reshape(bsz, ts, d)


def _out_proj(ya, yb, h, ga, gb, w):
    bsz, s, d = h.shape
    ca, cb = ya.shape[2], yb.shape[2]
    ts = OUT_TM // bsz
    return pl.pallas_call(
        _out_proj_kernel,
        grid=(s // ts, 1),
        in_specs=[
            pl.BlockSpec((ts, bsz, ca), lambda i, j: (i, 0, 0)),
            pl.BlockSpec((bsz, ts, cb), lambda i, j: (0, i, 0)),
            pl.BlockSpec((bsz, ts, d), lambda i, j: (0, i, 0)),
            pl.BlockSpec((1, ca), lambda i, j: (0, 0)),
            pl.BlockSpec((1, cb), lambda i, j: (0, 0)),
            pl.BlockSpec((ca, d), lambda i, j: (0, 0)),
            pl.BlockSpec((cb, d), lambda i, j: (1, 0)),
        ],
        out_specs=pl.BlockSpec((bsz, ts, d), lambda i, j: (0, i, 0)),
        out_shape=jax.ShapeDtypeStruct((bsz, s, d), jnp.float32),
        compiler_params=_vmem(48),
        name="out_proj",
    )(ya, yb, h, ga, gb, w, w)


def kernel(x, norm_ffn1, ffn1_w_in, ffn1_w_out, norm_mix, w_in_mix, lru_conv_w, lru_conv_b, lru_gate_w, lru_gate_b, lru_lambda, attn_rpb, lru_out_norm, attn_out_norm, w_out_mix, norm_ffn2, ffn2_w_in, ffn2_w_out, norm_final):
    bsz, s, d = x.shape
    depth = norm_ffn1.shape[0]
    assert depth >= 1 and d == D_MODEL
    bf16 = jnp.bfloat16
    t = bsz * s
    h = x.reshape(t, d)
    final_gain = norm_final.reshape(1, d)
    for l in range(depth):
        h, w_in_mix_bf, w_out_mix_bf = _ffn(
            h, norm_ffn1[l].reshape(1, d), ffn1_w_in[l].astype(bf16), ffn1_w_out[l].astype(bf16),
            final_gain, final_norm=False, cast=(w_in_mix[l], w_out_mix[l]))

        h3 = h.reshape(bsz, s, d)
        xg, qkv = _in_proj(h3, norm_mix[l].reshape(1, d), w_in_mix_bf)

        gate_w, gate_b = _lru_gate_weights(lru_gate_w[l], lru_gate_b[l])
        ya = _lru(xg, gate_w, gate_b, lru_conv_w[l], lru_conv_b[l].reshape(1, LRU_WIDTH),
                  lru_lambda[l].reshape(2, 1, LRU_WIDTH))
        yb, ffn2_w_in_bf, ffn2_w_out_bf = _attention(qkv, _attn_bias_table(attn_rpb[l]),
                                                     cast=(ffn2_w_in[l], ffn2_w_out[l]))

        h = _out_proj(ya, yb, h3, lru_out_norm[l].reshape(1, LRU_WIDTH),
                      attn_out_norm[l].reshape(1, ATTN_WIDTH), w_out_mix_bf).reshape(t, d)

        last = l == depth - 1
        h, = _ffn(h, norm_ffn2[l].reshape(1, d), ffn2_w_in_bf, ffn2_w_out_bf, final_gain, final_norm=last)
    return h.reshape(bsz, s, d)
```

```python
import functools

import jax
import jax.numpy as jnp
from jax import lax
from jax.experimental import pallas as pl
from jax.experimental.pallas import tpu as pltpu

D_MODEL = 2048
HEAD_DIM = 64
LRU_WIDTH = 1024
ATTN_WIDTH = 1024
ATTN_HEADS = ATTN_WIDTH // HEAD_DIM
IN_PROJ_WIDTH = 2 * LRU_WIDTH + 3 * ATTN_WIDTH
CONV_WIDTH = 4
CONV_PAD_L = CONV_WIDTH // 2
RG_C = 8.0
GRID_W = 64
WIN_ROWS = 8
WIN_COLS = 16
D_FF = 5632
NORM_EPS = 1e-6

LANES = 128
SUBLANES = 8
BF16_SUBLANES = 16
MASK_VALUE = -1e30

FFN_TM = 1024
FFN_TF = 512
PROJ_TM = 1024
PROJ_TN = 1024
OUT_TM = 512
LRU_TB = 256
SCAN_UNROLL = 8
ATTN_SLOTS = 4


def _vmem(mib):
    return pltpu.CompilerParams(
        dimension_semantics=("parallel", "arbitrary"), vmem_limit_bytes=mib * 1024 * 1024)


def _rms_norm(x, gain):
    ms = jnp.mean(x * x, axis=-1, keepdims=True)
    return x * lax.rsqrt(ms + NORM_EPS) * gain


def _ffn_kernel(*refs, final_norm, n_cast):
    x_ref, g_ref, wg_ref, wu_ref, wo_ref, gf_ref = refs[:6]
    cast_src = refs[6:6 + n_cast]
    o_ref = refs[6 + n_cast]
    cast_dst = refs[7 + n_cast:7 + 2 * n_cast]
    u_ref = refs[7 + 2 * n_cast]
    k = pl.program_id(1)

    @pl.when(k == 0)
    def _():
        x = x_ref[...]
        u_ref[...] = _rms_norm(x, g_ref[...]).astype(jnp.bfloat16)
        o_ref[...] = x

    _cast_blocks(cast_src, cast_dst)

    u = u_ref[...]
    gate = jnp.dot(u, wg_ref[...], preferred_element_type=jnp.float32)
    up = jnp.dot(u, wu_ref[...], preferred_element_type=jnp.float32)
    act = (gate * jax.nn.sigmoid(gate)) * (0.5 * up)
    o_ref[...] += jnp.dot(act.astype(jnp.bfloat16), wo_ref[...],
                          preferred_element_type=jnp.float32)

    if final_norm:
        @pl.when(k == pl.num_programs(1) - 1)
        def _():
            o_ref[...] = _rms_norm(o_ref[...], gf_ref[...])


def _cast_specs(arrays, n_steps, flat_step):
    specs, shapes = [], []
    for w in arrays:
        rows = BF16_SUBLANES
        while w.shape[0] % rows or w.shape[0] // rows > n_steps:
            rows += BF16_SUBLANES
        specs.append(pl.BlockSpec(
            (rows, w.shape[1]), lambda *idx, nb=w.shape[0] // rows: (jnp.minimum(flat_step(*idx), nb - 1), 0)))
        shapes.append(jax.ShapeDtypeStruct(w.shape, jnp.bfloat16))
    return specs, shapes


def _cast_blocks(srcs, dsts):
    for src, dst in zip(srcs, dsts):
        dst[...] = src[...].astype(dst.dtype)


def _ffn(h, gain, w_in, w_out, final_gain, final_norm, cast=()):
    t, d = h.shape
    nk = D_FF // FFN_TF
    cast_specs, cast_shapes = _cast_specs(cast, (t // FFN_TM) * nk, lambda i, k: i * nk + k)
    return pl.pallas_call(
        functools.partial(_ffn_kernel, final_norm=final_norm, n_cast=len(cast)),
        grid=(t // FFN_TM, nk),
        in_specs=[
            pl.BlockSpec((FFN_TM, d), lambda i, k: (i, 0)),
            pl.BlockSpec((1, d), lambda i, k: (0, 0)),
            pl.BlockSpec((d, FFN_TF), lambda i, k: (0, k)),
            pl.BlockSpec((d, FFN_TF), lambda i, k: (0, k + nk)),
            pl.BlockSpec((FFN_TF, d), lambda i, k: (k, 0)),
            pl.BlockSpec((1, d), lambda i, k: (0, 0)),
        ] + cast_specs,
        out_specs=[pl.BlockSpec((FFN_TM, d), lambda i, k: (i, 0))] + cast_specs,
        out_shape=[jax.ShapeDtypeStruct((t, d), jnp.float32)] + cast_shapes,
        scratch_shapes=[pltpu.VMEM((FFN_TM, d), jnp.bfloat16)],
        compiler_params=_vmem(58),
        name="ffn_final" if final_norm else "ffn",
    )(h, gain, w_in, w_in, w_out, final_gain, *cast)


def _in_proj_kernel(x_ref, g_ref, w_ref, lru_ref, qkv_ref, u_ref, *, n_lru):
    n = pl.program_id(1)
    bsz, ts, d = x_ref.shape

    @pl.when(n == 0)
    def _():
        u_ref[...] = _rms_norm(x_ref[...].reshape(bsz * ts, d), g_ref[...]).astype(jnp.bfloat16)

    res = jnp.dot(u_ref[...], w_ref[...], preferred_element_type=jnp.float32).reshape(bsz, ts, -1)

    @pl.when(n < n_lru)
    def _():
        lru_ref[...] = jnp.swapaxes(res, 0, 1)

    @pl.when(n >= n_lru)
    def _():
        qkv_ref[...] = res.astype(qkv_ref.dtype)


def _in_proj(h, gain, w):
    bsz, s, d = h.shape
    ts = PROJ_TM // bsz
    n_lru = 2 * LRU_WIDTH // PROJ_TN
    n_qkv = 3 * ATTN_WIDTH // PROJ_TN
    return pl.pallas_call(
        functools.partial(_in_proj_kernel, n_lru=n_lru),
        grid=(s // ts, n_lru + n_qkv),
        in_specs=[
            pl.BlockSpec((bsz, ts, d), lambda i, n: (0, i, 0)),
            pl.BlockSpec((1, d), lambda i, n: (0, 0)),
            pl.BlockSpec((d, PROJ_TN), lambda i, n: (0, n)),
        ],
        out_specs=[
            pl.BlockSpec((ts, bsz, PROJ_TN), lambda i, n: (i, 0, jnp.minimum(n, n_lru - 1))),
            pl.BlockSpec((bsz, ts, PROJ_TN), lambda i, n: (0, i, jnp.maximum(n - n_lru, 0))),
        ],
        out_shape=[
            jax.ShapeDtypeStruct((s, bsz, 2 * LRU_WIDTH), jnp.float32),
            jax.ShapeDtypeStruct((bsz, s, 3 * ATTN_WIDTH), jnp.bfloat16),
        ],
        scratch_shapes=[pltpu.VMEM((PROJ_TM, d), jnp.bfloat16)],
        compiler_params=_vmem(56),
        name="in_proj",
    )(h, gain, w)


def _lru_kernel(x_ref, gf_ref, gr_ref, w_ref, gb_ref, cw_ref, cb_ref, lam_ref, o_ref,
                xw_scr, a_scr, b_scr, hs_scr, h_scr, *, n_chunks):
    j = pl.program_id(1)
    seq = x_ref.shape[0]
    rows = LRU_TB * SUBLANES
    n_tail = CONV_WIDTH - 1 - CONV_PAD_L
    second_visit = j >= n_chunks // 2

    @pl.when(j == 0)
    def _():
        h_scr[...] = jnp.zeros_like(h_scr)

    starts = []
    for d in range(2):
        chunk = (n_chunks - 1 - j) if d else j
        t0 = pl.multiple_of(chunk * LRU_TB, LRU_TB)
        starts.append(t0)
        xw_scr[d, CONV_PAD_L:CONV_PAD_L + LRU_TB] = x_ref[pl.ds(t0, LRU_TB)]
        head = x_ref[pl.ds(jnp.maximum(t0 - CONV_PAD_L, 0), CONV_PAD_L)]
        xw_scr[d, :CONV_PAD_L] = jnp.where(chunk > 0, head, 0.0)
        tail = x_ref[pl.ds(jnp.minimum(t0 + LRU_TB, seq - n_tail), n_tail)]
        xw_scr[d, CONV_PAD_L + LRU_TB:CONV_PAD_L + LRU_TB + n_tail] = jnp.where(chunk < n_chunks - 1, tail, 0.0)

        xc = cb_ref[...][None]
        for tap in range(CONV_WIDTH):
            xc = xc + cw_ref[tap:tap + 1, :][None] * xw_scr[d, tap:tap + LRU_TB]
        xc2 = xc.reshape(rows, LANES)
        pre = jnp.dot(xc2.astype(jnp.bfloat16), w_ref[d, 0], preferred_element_type=jnp.float32)
        pre = pre + gb_ref[d, 0]
        r = jax.nn.sigmoid(pre[:, :LANES])
        i = jax.nn.sigmoid(pre[:, LANES:])
        z = -lam_ref[d]
        softplus = jnp.maximum(z, 0.0) + jnp.log1p(jnp.exp(-jnp.abs(z)))
        log_a = (-RG_C * softplus) * r
        a = jnp.exp(log_a)
        sq = -jnp.tanh(log_a) * (a * a + 1.0)
        b = jnp.where(sq > 0.0, sq * lax.rsqrt(sq), 0.0) * (i * xc2)
        a_scr[d] = a.reshape(LRU_TB, SUBLANES, LANES)
        b_scr[d] = b.reshape(LRU_TB, SUBLANES, LANES)

    def step(s, carry):
        hf, hr = carry
        tr = LRU_TB - 1 - s
        hf = a_scr[0, s] * hf + b_scr[0, s]
        hr = a_scr[1, tr] * hr + b_scr[1, tr]
        hs_scr[0, s] = hf
        hs_scr[1, tr] = hr
        return hf, hr

    hf, hr = lax.fori_loop(0, LRU_TB, step, (h_scr[0], h_scr[1]), unroll=SCAN_UNROLL)
    h_scr[0] = hf
    h_scr[1] = hr

    for d, g_ref in enumerate((gf_ref, gr_ref)):
        window = pl.ds(starts[d], LRU_TB)

        @pl.when(jnp.logical_not(second_visit))
        def _():
            o_ref[window] = hs_scr[d]

        @pl.when(second_visit)
        def _():
            o_ref[window] = jax.nn.gelu(g_ref[...], approximate=True) * (o_ref[window] + hs_scr[d])


def _lru(xg, gate_w, gate_b, conv_w, conv_b, lam):
    s, bsz, c2 = xg.shape
    c = c2 // 2
    n_cblk = c // LANES
    n_chunks = s // LRU_TB
    half = n_chunks // 2
    assert n_chunks % 2 == 0
    gate_fwd = pl.BlockSpec((LRU_TB, bsz, LANES), lambda cb, j: (jnp.maximum(j, half), 0, n_cblk + cb))
    gate_rev = pl.BlockSpec((LRU_TB, bsz, LANES),
                            lambda cb, j: (jnp.minimum(n_chunks - 1 - j, half - 1), 0, n_cblk + cb))
    return pl.pallas_call(
        functools.partial(_lru_kernel, n_chunks=n_chunks),
        grid=(n_cblk, n_chunks),
        in_specs=[
            pl.BlockSpec((s, bsz, LANES), lambda cb, j: (0, 0, cb)),
            gate_fwd,
            gate_rev,
            pl.BlockSpec((2, 1, LANES, 2 * LANES), lambda cb, j: (0, cb, 0, 0)),
            pl.BlockSpec((2, 1, 1, 2 * LANES), lambda cb, j: (0, cb, 0, 0)),
            pl.BlockSpec((CONV_WIDTH, LANES), lambda cb, j: (0, cb)),
            pl.BlockSpec((1, LANES), lambda cb, j: (0, cb)),
            pl.BlockSpec((2, 1, LANES), lambda cb, j: (0, 0, cb)),
        ],
        out_specs=pl.BlockSpec((s, bsz, LANES), lambda cb, j: (0, 0, cb)),
        out_shape=jax.ShapeDtypeStruct((s, bsz, c), jnp.float32),
        scratch_shapes=[
            pltpu.VMEM((2, LRU_TB + SUBLANES, bsz, LANES), jnp.float32),
            pltpu.VMEM((2, LRU_TB, bsz, LANES), jnp.float32),
            pltpu.VMEM((2, LRU_TB, bsz, LANES), jnp.float32),
            pltpu.VMEM((2, LRU_TB, bsz, LANES), jnp.float32),
            pltpu.VMEM((2, bsz, LANES), jnp.float32),
        ],
        compiler_params=_vmem(56),
        name="lru",
    )(xg, xg, xg, gate_w, gate_b, conv_w, conv_b, lam)


def _lru_gate_weights(gate_w, gate_b):
    n_cblk = LRU_WIDTH // LANES
    hpb = LANES // HEAD_DIM
    w = gate_w.reshape(2, 2, n_cblk, hpb, HEAD_DIM, HEAD_DIM)
    eye = jnp.eye(hpb, dtype=gate_w.dtype)
    wbd = w[:, :, :, :, :, None, :] * eye[None, None, None, :, None, :, None]
    wbd = wbd.reshape(2, 2, n_cblk, LANES, LANES)
    wbd = wbd.transpose(0, 2, 3, 1, 4).reshape(2, n_cblk, LANES, 2 * LANES)
    bias = gate_b.reshape(2, 2, n_cblk, LANES).transpose(0, 2, 1, 3).reshape(2, n_cblk, 1, 2 * LANES)
    return wbd.astype(jnp.bfloat16), bias


def _attn_kernel(*refs, rows, kr, n_cast):
    q_ref, k_ref, v_ref, rows_ref = refs[:4]
    o_ref = refs[4 + n_cast]
    q_scr, v_scr, p_scr, bias_scr = refs[5 + 2 * n_cast:]
    _cast_blocks(refs[4:4 + n_cast], refs[5 + n_cast:5 + 2 * n_cast])

    @pl.when(pl.program_id(1) == 0)
    def _():
        _build_bias_table(rows_ref, bias_scr)

    lane = lax.broadcasted_iota(jnp.int32, (1, 1, LANES), 2)
    first_head = lane < HEAD_DIM
    q = q_ref[0].reshape(rows, GRID_W, LANES) * HEAD_DIM ** -0.5
    for r in range(rows):
        stacked = jnp.concatenate([jnp.where(first_head[0], q[r], 0.0), jnp.where(first_head[0], 0.0, q[r])], axis=0)
        q_scr[r] = stacked.T
    v_scr[:, :LANES] = v_ref[0]
    v_scr[:, LANES:] = jnp.ones((rows * GRID_W, LANES), jnp.bfloat16)
    band = kr * GRID_W

    def band_start(r):
        return min(max(r - kr // 2, 0), rows - kr)

    def scores(r):
        rs = band_start(r)
        s_t = jnp.dot(k_ref[0, rs * GRID_W:rs * GRID_W + band, :], q_scr[r],
                      preferred_element_type=jnp.float32)
        s_t = s_t + jnp.concatenate([bias_scr[rs + i - r + WIN_ROWS - 1] for i in range(kr)], axis=0)
        return s_t

    def exponentials(r, s_t):
        m = jnp.max(s_t, axis=0, keepdims=True)
        p_scr[r % ATTN_SLOTS] = jnp.exp(s_t - m).astype(jnp.bfloat16)

    def weighted_values(r):
        rs = band_start(r)
        pv = lax.dot_general(p_scr[r % ATTN_SLOTS], v_scr[rs * GRID_W:rs * GRID_W + band, :],
                             (((0,), (0,)), ((), ())), preferred_element_type=jnp.float32)
        out = pv[:, :LANES] / pv[:, LANES:]
        o_ref[0, r * GRID_W:(r + 1) * GRID_W, :] = jnp.where(first_head[0], out[:GRID_W], out[GRID_W:])

    for step in range(rows + 2):
        if step >= 2:
            weighted_values(step - 2)
        if 1 <= step <= rows:
            exponentials(step - 1, pending)
        if step < rows:
            pending = scores(step)


def _attention(proj, bias, cast=()):
    bsz, s, _ = proj.shape
    rows = s // GRID_W
    kr = min(WIN_ROWS, rows)
    n_hp = ATTN_WIDTH // LANES
    q_blk = 0
    k_blk = q_blk + n_hp
    v_blk = k_blk + n_hp
    blk = (1, s, LANES)
    cast_specs, cast_shapes = _cast_specs(cast, n_hp * bsz, lambda hp, b: hp * bsz + b)
    return pl.pallas_call(
        functools.partial(_attn_kernel, rows=rows, kr=kr, n_cast=len(cast)),
        grid=(n_hp, bsz),
        in_specs=[
            pl.BlockSpec(blk, lambda hp, b: (b, 0, q_blk + hp)),
            pl.BlockSpec(blk, lambda hp, b: (b, 0, k_blk + hp)),
            pl.BlockSpec(blk, lambda hp, b: (b, 0, v_blk + hp)),
            pl.BlockSpec((2, 2 * WIN_ROWS - 1, LANES), lambda hp, b: (hp, 0, 0)),
        ] + cast_specs,
        out_specs=[pl.BlockSpec(blk, lambda hp, b: (b, 0, hp))] + cast_specs,
        out_shape=[jax.ShapeDtypeStruct((bsz, s, ATTN_WIDTH), jnp.float32)] + cast_shapes,
        scratch_shapes=[
            pltpu.VMEM((rows, 2 * GRID_W, LANES), jnp.bfloat16),
            pltpu.VMEM((s, 2 * LANES), jnp.bfloat16),
            pltpu.VMEM((ATTN_SLOTS, kr * GRID_W, LANES), jnp.bfloat16),
            pltpu.VMEM((2 * WIN_ROWS - 1, GRID_W, LANES), jnp.float32),
        ],
        compiler_params=_vmem(32),
        name="attn",
    )(proj, proj, proj, bias, *cast)


def _attn_bias_rows(rpb):
    return jnp.pad(rpb[:, :, ::-1], ((0, 0), (0, 0), (0, LANES - rpb.shape[2])), constant_values=MASK_VALUE)


def _build_bias_table(rows_ref, bias_scr):
    shape = (GRID_W, LANES)
    kc = lax.broadcasted_iota(jnp.int32, shape, 0)
    lane = lax.broadcasted_iota(jnp.int32, shape, 1)
    second_head = lane >= GRID_W
    c = jnp.where(second_head, lane - GRID_W, lane)
    start = jnp.clip(c - WIN_COLS // 2, 0, GRID_W - WIN_COLS)
    inside = (kc >= start) & (kc < start + WIN_COLS)
    for d in range(bias_scr.shape[0]):
        t0 = pltpu.roll(jnp.broadcast_to(rows_ref[0, d:d + 1, :], shape), LANES - (WIN_COLS - 1), 1,
                        stride=1, stride_axis=0)
        t1 = pltpu.roll(jnp.broadcast_to(rows_ref[1, d:d + 1, :], shape), GRID_W - (WIN_COLS - 1), 1,
                        stride=1, stride_axis=0)
        bias_scr[d] = jnp.where(inside, jnp.where(second_head, t1, t0), MASK_VALUE)


def _out_proj_kernel(ya_ref, yb_ref, h_ref, ga_ref, gb_ref, wa_ref, wb_ref, o_ref):
    bsz, ts, d = h_ref.shape
    ya = jnp.swapaxes(ya_ref[...], 0, 1).reshape(bsz * ts, -1)
    yb = yb_ref[...].reshape(bsz * ts, -1)
    ua = _rms_norm(ya, ga_ref[...]).astype(jnp.bfloat16)
    ub = _rms_norm(yb, gb_ref[...]).astype(jnp.bfloat16)
    out = (h_ref[...].reshape(bsz * ts, d)
           + jnp.dot(ua, wa_ref[...], preferred_element_type=jnp.float32)
           + jnp.dot(ub, wb_ref[...], preferred_element_type=jnp.float32))
    o_ref[...] = out.reshape(bsz, ts, d)


def _out_proj(ya, yb, h, ga, gb, w):
    bsz, s, d = h.shape
    ca, cb = ya.shape[2], yb.shape[2]
    ts = OUT_TM // bsz
    return pl.pallas_call(
        _out_proj_kernel,
        grid=(s // ts, 1),
        in_specs=[
            pl.BlockSpec((ts, bsz, ca), lambda i, j: (i, 0, 0)),
            pl.BlockSpec((bsz, ts, cb), lambda i, j: (0, i, 0)),
            pl.BlockSpec((bsz, ts, d), lambda i, j: (0, i, 0)),
            pl.BlockSpec((1, ca), lambda i, j: (0, 0)),
            pl.BlockSpec((1, cb), lambda i, j: (0, 0)),
            pl.BlockSpec((ca, d), lambda i, j: (0, 0)),
            pl.BlockSpec((cb, d), lambda i, j: (1, 0)),
        ],
        out_specs=pl.BlockSpec((bsz, ts, d), lambda i, j: (0, i, 0)),
        out_shape=jax.ShapeDtypeStruct((bsz, s, d), jnp.float32),
        compiler_params=_vmem(48),
        name="out_proj",
    )(ya, yb, h, ga, gb, w, w)


def kernel(x, norm_ffn1, ffn1_w_in, ffn1_w_out, norm_mix, w_in_mix, lru_conv_w, lru_conv_b, lru_gate_w, lru_gate_b, lru_lambda, attn_rpb, lru_out_norm, attn_out_norm, w_out_mix, norm_ffn2, ffn2_w_in, ffn2_w_out, norm_final):
    bsz, s, d = x.shape
    depth = norm_ffn1.shape[0]
    assert depth >= 1 and d == D_MODEL
    bf16 = jnp.bfloat16
    t = bsz * s
    h = x.reshape(t, d)
    final_gain = norm_final.reshape(1, d)
    for l in range(depth):
        h, w_in_mix_bf, w_out_mix_bf = _ffn(
            h, norm_ffn1[l].reshape(1, d), ffn1_w_in[l].astype(bf16), ffn1_w_out[l].astype(bf16),
            final_gain, final_norm=False, cast=(w_in_mix[l], w_out_mix[l]))

        h3 = h.reshape(bsz, s, d)
        xg, qkv = _in_proj(h3, norm_mix[l].reshape(1, d), w_in_mix_bf)

        gate_w, gate_b = _lru_gate_weights(lru_gate_w[l], lru_gate_b[l])
        ya = _lru(xg, gate_w, gate_b, lru_conv_w[l], lru_conv_b[l].reshape(1, LRU_WIDTH),
                  lru_lambda[l].reshape(2, 1, LRU_WIDTH))
        yb, ffn2_w_in_bf, ffn2_w_out_bf = _attention(qkv, _attn_bias_rows(attn_rpb[l]),
                                                     cast=(ffn2_w_in[l], ffn2_w_out[l]))

        h = _out_proj(ya, yb, h3, lru_out_norm[l].reshape(1, LRU_WIDTH),
                      attn_out_norm[l].reshape(1, ATTN_WIDTH), w_out_mix_bf).reshape(t, d)

        last = l == depth - 1
        h, = _ffn(h, norm_ffn2[l].reshape(1, d), ffn2_w_in_bf, ffn2_w_out_bf, final_gain, final_norm=last)
    return h.reshape(bsz, s, d)
```

```python
import functools

import jax
import jax.numpy as jnp
from jax import lax
from jax.experimental import pallas as pl
from jax.experimental.pallas import tpu as pltpu

D_MODEL = 2048
HEAD_DIM = 64
LRU_WIDTH = 1024
ATTN_WIDTH = 1024
ATTN_HEADS = ATTN_WIDTH // HEAD_DIM
IN_PROJ_WIDTH = 2 * LRU_WIDTH + 3 * ATTN_WIDTH
CONV_WIDTH = 4
CONV_PAD_L = CONV_WIDTH // 2
RG_C = 8.0
GRID_W = 64
WIN_ROWS = 8
WIN_COLS = 16
D_FF = 5632
NORM_EPS = 1e-6

LANES = 128
SUBLANES = 8
BF16_SUBLANES = 16
MASK_VALUE = -1e30

FFN_TM = 1024
FFN_TF = 512
PROJ_TM = 1024
PROJ_TN = 1024
OUT_TM = 512
LRU_TB = 256
SCAN_UNROLL = 8
ATTN_SLOTS = 4


def _vmem(mib):
    return pltpu.CompilerParams(
        dimension_semantics=("arbitrary", "arbitrary"), vmem_limit_bytes=mib * 1024 * 1024)


def _rms_norm(x, gain):
    ms = jnp.mean(x * x, axis=-1, keepdims=True)
    return x * lax.rsqrt(ms + NORM_EPS) * gain


def _ffn_kernel(*refs, final_norm, n_cast):
    x_ref, g_ref, wg_ref, wu_ref, wo_ref, gf_ref = refs[:6]
    cast_src = refs[6:6 + n_cast]
    o_ref = refs[6 + n_cast]
    cast_dst = refs[7 + n_cast:7 + 2 * n_cast]
    u_ref = refs[7 + 2 * n_cast]
    k = pl.program_id(1)

    def step(first):
        _cast_blocks(cast_src, cast_dst)
        if first:
            u = _rms_norm(x_ref[...], g_ref[...]).astype(jnp.bfloat16)
            u_ref[...] = u
        else:
            u = u_ref[...]
        gate = jnp.dot(u, wg_ref[...], preferred_element_type=jnp.float32)
        up = jnp.dot(u, wu_ref[...], preferred_element_type=jnp.float32)
        act = (gate * jax.nn.sigmoid(gate)) * (0.5 * up)
        contrib = jnp.dot(act.astype(jnp.bfloat16), wo_ref[...], preferred_element_type=jnp.float32)
        o_ref[...] = (x_ref[...] if first else o_ref[...]) + contrib

    pl.when(k == 0)(functools.partial(step, True))
    pl.when(k > 0)(functools.partial(step, False))

    if final_norm:
        @pl.when(k == pl.num_programs(1) - 1)
        def _():
            o_ref[...] = _rms_norm(o_ref[...], gf_ref[...])


def _cast_specs(arrays, n_steps, flat_step):
    specs, shapes = [], []
    for w in arrays:
        rows = BF16_SUBLANES
        while w.shape[0] % rows or w.shape[0] // rows > n_steps:
            rows += BF16_SUBLANES
        specs.append(pl.BlockSpec(
            (rows, w.shape[1]), lambda *idx, nb=w.shape[0] // rows: (jnp.minimum(flat_step(*idx), nb - 1), 0)))
        shapes.append(jax.ShapeDtypeStruct(w.shape, jnp.bfloat16))
    return specs, shapes


def _cast_blocks(srcs, dsts):
    for src, dst in zip(srcs, dsts):
        dst[...] = src[...].astype(dst.dtype)


def _ffn(h, gain, w_in, w_out, final_gain, final_norm, cast=()):
    t, d = h.shape
    nk = D_FF // FFN_TF
    cast_specs, cast_shapes = _cast_specs(cast, (t // FFN_TM) * nk, lambda i, k: i * nk + k)
    return pl.pallas_call(
        functools.partial(_ffn_kernel, final_norm=final_norm, n_cast=len(cast)),
        grid=(t // FFN_TM, nk),
        in_specs=[
            pl.BlockSpec((FFN_TM, d), lambda i, k: (i, 0)),
            pl.BlockSpec((1, d), lambda i, k: (0, 0)),
            pl.BlockSpec((d, FFN_TF), lambda i, k: (0, k)),
            pl.BlockSpec((d, FFN_TF), lambda i, k: (0, k + nk)),
            pl.BlockSpec((FFN_TF, d), lambda i, k: (k, 0)),
            pl.BlockSpec((1, d), lambda i, k: (0, 0)),
        ] + cast_specs,
        out_specs=[pl.BlockSpec((FFN_TM, d), lambda i, k: (i, 0))] + cast_specs,
        out_shape=[jax.ShapeDtypeStruct((t, d), jnp.float32)] + cast_shapes,
        scratch_shapes=[pltpu.VMEM((FFN_TM, d), jnp.bfloat16)],
        compiler_params=_vmem(58),
        name="ffn_final" if final_norm else "ffn",
    )(h, gain, w_in, w_in, w_out, final_gain, *cast)


def _in_proj_kernel(x_ref, g_ref, w_ref, lru_ref, qkv_ref, u_ref, *, n_lru):
    n = pl.program_id(1)
    bsz, ts, d = x_ref.shape

    def step(first, time_major):
        if first:
            u = _rms_norm(x_ref[...].reshape(bsz * ts, d), g_ref[...]).astype(jnp.bfloat16)
            u_ref[...] = u
        else:
            u = u_ref[...]
        res = jnp.dot(u, w_ref[...], preferred_element_type=jnp.float32).reshape(bsz, ts, -1)
        if time_major:
            lru_ref[...] = jnp.swapaxes(res, 0, 1)
        else:
            qkv_ref[...] = res.astype(qkv_ref.dtype)

    pl.when(n == 0)(functools.partial(step, True, True))
    pl.when((n > 0) & (n < n_lru))(functools.partial(step, False, True))
    pl.when(n >= n_lru)(functools.partial(step, False, False))


def _in_proj(h, gain, w):
    bsz, s, d = h.shape
    ts = PROJ_TM // bsz
    n_lru = 2 * LRU_WIDTH // PROJ_TN
    n_qkv = 3 * ATTN_WIDTH // PROJ_TN
    return pl.pallas_call(
        functools.partial(_in_proj_kernel, n_lru=n_lru),
        grid=(s // ts, n_lru + n_qkv),
        in_specs=[
            pl.BlockSpec((bsz, ts, d), lambda i, n: (0, i, 0)),
            pl.BlockSpec((1, d), lambda i, n: (0, 0)),
            pl.BlockSpec((d, PROJ_TN), lambda i, n: (0, n)),
        ],
        out_specs=[
            pl.BlockSpec((ts, bsz, PROJ_TN), lambda i, n: (i, 0, jnp.minimum(n, n_lru - 1))),
            pl.BlockSpec((bsz, ts, PROJ_TN), lambda i, n: (0, i, jnp.maximum(n - n_lru, 0))),
        ],
        out_shape=[
            jax.ShapeDtypeStruct((s, bsz, 2 * LRU_WIDTH), jnp.float32),
            jax.ShapeDtypeStruct((bsz, s, 3 * ATTN_WIDTH), jnp.bfloat16),
        ],
        scratch_shapes=[pltpu.VMEM((PROJ_TM, d), jnp.bfloat16)],
        compiler_params=_vmem(56),
        name="in_proj",
    )(h, gain, w)


def _lru_kernel(x_ref, gf_ref, gr_ref, w_ref, gb_ref, cw_ref, cb_ref, lam_ref, o_ref,
                xw_scr, a_scr, b_scr, hs_scr, h_scr, *, n_chunks):
    j = pl.program_id(1)
    seq = x_ref.shape[0]
    rows = LRU_TB * SUBLANES
    n_tail = CONV_WIDTH - 1 - CONV_PAD_L
    second_visit = j >= n_chunks // 2

    @pl.when(j == 0)
    def _():
        h_scr[...] = jnp.zeros_like(h_scr)

    starts = []
    for d in range(2):
        chunk = (n_chunks - 1 - j) if d else j
        t0 = pl.multiple_of(chunk * LRU_TB, LRU_TB)
        starts.append(t0)
        xw_scr[d, CONV_PAD_L:CONV_PAD_L + LRU_TB] = x_ref[pl.ds(t0, LRU_TB)]
        head = x_ref[pl.ds(jnp.maximum(t0 - CONV_PAD_L, 0), CONV_PAD_L)]
        xw_scr[d, :CONV_PAD_L] = jnp.where(chunk > 0, head, 0.0)
        tail = x_ref[pl.ds(jnp.minimum(t0 + LRU_TB, seq - n_tail), n_tail)]
        xw_scr[d, CONV_PAD_L + LRU_TB:CONV_PAD_L + LRU_TB + n_tail] = jnp.where(chunk < n_chunks - 1, tail, 0.0)

        xc = cb_ref[...][None]
        for tap in range(CONV_WIDTH):
            xc = xc + cw_ref[tap:tap + 1, :][None] * xw_scr[d, tap:tap + LRU_TB]
        xc2 = xc.reshape(rows, LANES)
        pre = jnp.dot(xc2.astype(jnp.bfloat16), w_ref[d, 0], preferred_element_type=jnp.float32)
        pre = pre + gb_ref[d, 0]
        r = jax.nn.sigmoid(pre[:, :LANES])
        i = jax.nn.sigmoid(pre[:, LANES:])
        z = -lam_ref[d]
        softplus = jnp.maximum(z, 0.0) + jnp.log1p(jnp.exp(-jnp.abs(z)))
        log_a = (-RG_C * softplus) * r
        a = jnp.exp(log_a)
        sq = -jnp.tanh(log_a) * (a * a + 1.0)
        b = jnp.where(sq > 0.0, sq * lax.rsqrt(sq), 0.0) * (i * xc2)
        a_scr[d] = a.reshape(LRU_TB, SUBLANES, LANES)
        b_scr[d] = b.reshape(LRU_TB, SUBLANES, LANES)

    def step(s, carry):
        hf, hr = carry
        tr = LRU_TB - 1 - s
        hf = a_scr[0, s] * hf + b_scr[0, s]
        hr = a_scr[1, tr] * hr + b_scr[1, tr]
        hs_scr[0, s] = hf
        hs_scr[1, tr] = hr
        return hf, hr

    hf, hr = lax.fori_loop(0, LRU_TB, step, (h_scr[0], h_scr[1]), unroll=SCAN_UNROLL)
    h_scr[0] = hf
    h_scr[1] = hr

    for d, g_ref in enumerate((gf_ref, gr_ref)):
        window = pl.ds(starts[d], LRU_TB)

        @pl.when(jnp.logical_not(second_visit))
        def _():
            o_ref[window] = hs_scr[d]

        @pl.when(second_visit)
        def _():
            o_ref[window] = jax.nn.gelu(g_ref[...], approximate=True) * (o_ref[window] + hs_scr[d])


def _lru(xg, gate_w, gate_b, conv_w, conv_b, lam):
    s, bsz, c2 = xg.shape
    c = c2 // 2
    n_cblk = c // LANES
    n_chunks = s // LRU_TB
    half = n_chunks // 2
    assert n_chunks % 2 == 0
    gate_fwd = pl.BlockSpec((LRU_TB, bsz, LANES), lambda cb, j: (jnp.maximum(j, half), 0, n_cblk + cb))
    gate_rev = pl.BlockSpec((LRU_TB, bsz, LANES),
                            lambda cb, j: (jnp.minimum(n_chunks - 1 - j, half - 1), 0, n_cblk + cb))
    return pl.pallas_call(
        functools.partial(_lru_kernel, n_chunks=n_chunks),
        grid=(n_cblk, n_chunks),
        in_specs=[
            pl.BlockSpec((s, bsz, LANES), lambda cb, j: (0, 0, cb)),
            gate_fwd,
            gate_rev,
            pl.BlockSpec((2, 1, LANES, 2 * LANES), lambda cb, j: (0, cb, 0, 0)),
            pl.BlockSpec((2, 1, 1, 2 * LANES), lambda cb, j: (0, cb, 0, 0)),
            pl.BlockSpec((CONV_WIDTH, LANES), lambda cb, j: (0, cb)),
            pl.BlockSpec((1, LANES), lambda cb, j: (0, cb)),
            pl.BlockSpec((2, 1, LANES), lambda cb, j: (0, 0, cb)),
        ],
        out_specs=pl.BlockSpec((s, bsz, LANES), lambda cb, j: (0, 0, cb)),
        out_shape=jax.ShapeDtypeStruct((s, bsz, c), jnp.float32),
        scratch_shapes=[
            pltpu.VMEM((2, LRU_TB + SUBLANES, bsz, LANES), jnp.float32),
            pltpu.VMEM((2, LRU_TB, bsz, LANES), jnp.float32),
            pltpu.VMEM((2, LRU_TB, bsz, LANES), jnp.float32),
            pltpu.VMEM((2, LRU_TB, bsz, LANES), jnp.float32),
            pltpu.VMEM((2, bsz, LANES), jnp.float32),
        ],
        compiler_params=_vmem(56),
        name="lru",
    )(xg, xg, xg, gate_w, gate_b, conv_w, conv_b, lam)


def _lru_gate_weights(gate_w, gate_b):
    n_cblk = LRU_WIDTH // LANES
    hpb = LANES // HEAD_DIM
    w = gate_w.reshape(2, 2, n_cblk, hpb, HEAD_DIM, HEAD_DIM)
    eye = jnp.eye(hpb, dtype=gate_w.dtype)
    wbd = w[:, :, :, :, :, None, :] * eye[None, None, None, :, None, :, None]
    wbd = wbd.reshape(2, 2, n_cblk, LANES, LANES)
    wbd = wbd.transpose(0, 2, 3, 1, 4).reshape(2, n_cblk, LANES, 2 * LANES)
    bias = gate_b.reshape(2, 2, n_cblk, LANES).transpose(0, 2, 1, 3).reshape(2, n_cblk, 1, 2 * LANES)
    return wbd.astype(jnp.bfloat16), bias


def _attn_kernel(*refs, rows, kr, n_cast):
    q_ref, k_ref, v_ref, rows_ref = refs[:4]
    o_ref = refs[4 + n_cast]
    q_scr, v_scr, p_scr, bias_scr = refs[5 + 2 * n_cast:]
    _cast_blocks(refs[4:4 + n_cast], refs[5 + n_cast:5 + 2 * n_cast])

    @pl.when(pl.program_id(1) == 0)
    def _():
        _build_bias_table(rows_ref, bias_scr)

    lane = lax.broadcasted_iota(jnp.int32, (1, 1, LANES), 2)
    first_head = lane < HEAD_DIM
    q = q_ref[0].reshape(rows, GRID_W, LANES) * HEAD_DIM ** -0.5
    for r in range(rows):
        stacked = jnp.concatenate([jnp.where(first_head[0], q[r], 0.0), jnp.where(first_head[0], 0.0, q[r])], axis=0)
        q_scr[r] = stacked.T
    v_scr[:, :LANES] = v_ref[0]
    v_scr[:, LANES:] = jnp.ones((rows * GRID_W, LANES), jnp.bfloat16)
    band = kr * GRID_W

    def band_start(r):
        return min(max(r - kr // 2, 0), rows - kr)

    def scores(r):
        rs = band_start(r)
        s_t = jnp.dot(k_ref[0, rs * GRID_W:rs * GRID_W + band, :], q_scr[r],
                      preferred_element_type=jnp.float32)
        s_t = s_t + jnp.concatenate([bias_scr[rs + i - r + WIN_ROWS - 1] for i in range(kr)], axis=0)
        return s_t

    def exponentials(r, s_t):
        m = jnp.max(s_t, axis=0, keepdims=True)
        p_scr[r % ATTN_SLOTS] = jnp.exp(s_t - m).astype(jnp.bfloat16)

    def weighted_values(r):
        rs = band_start(r)
        pv = lax.dot_general(p_scr[r % ATTN_SLOTS], v_scr[rs * GRID_W:rs * GRID_W + band, :],
                             (((0,), (0,)), ((), ())), preferred_element_type=jnp.float32)
        out = pv[:, :LANES] / pv[:, LANES:]
        o_ref[0, r * GRID_W:(r + 1) * GRID_W, :] = jnp.where(first_head[0], out[:GRID_W], out[GRID_W:])

    for step in range(rows + 2):
        if step >= 2:
            weighted_values(step - 2)
        if 1 <= step <= rows:
            exponentials(step - 1, pending)
        if step < rows:
            pending = scores(step)


def _attention(proj, bias, cast=()):
    bsz, s, _ = proj.shape
    rows = s // GRID_W
    kr = min(WIN_ROWS, rows)
    n_hp = ATTN_WIDTH // LANES
    q_blk = 0
    k_blk = q_blk + n_hp
    v_blk = k_blk + n_hp
    blk = (1, s, LANES)
    cast_specs, cast_shapes = _cast_specs(cast, n_hp * bsz, lambda hp, b: hp * bsz + b)
    return pl.pallas_call(
        functools.partial(_attn_kernel, rows=rows, kr=kr, n_cast=len(cast)),
        grid=(n_hp, bsz),
        in_specs=[
            pl.BlockSpec(blk, lambda hp, b: (b, 0, q_blk + hp)),
            pl.BlockSpec(blk, lambda hp, b: (b, 0, k_blk + hp)),
            pl.BlockSpec(blk, lambda hp, b: (b, 0, v_blk + hp)),
            pl.BlockSpec((2, 2 * WIN_ROWS - 1, LANES), lambda hp, b: (hp, 0, 0)),
        ] + cast_specs,
        out_specs=[pl.BlockSpec(blk, lambda hp, b: (b, 0, hp))] + cast_specs,
        out_shape=[jax.ShapeDtypeStruct((bsz, s, ATTN_WIDTH), jnp.float32)] + cast_shapes,
        scratch_shapes=[
            pltpu.VMEM((rows, 2 * GRID_W, LANES), jnp.bfloat16),
            pltpu.VMEM((s, 2 * LANES), jnp.bfloat16),
            pltpu.VMEM((ATTN_SLOTS, kr * GRID_W, LANES), jnp.bfloat16),
            pltpu.VMEM((2 * WIN_ROWS - 1, GRID_W, LANES), jnp.float32),
        ],
        compiler_params=_vmem(32),
        name="attn",
    )(proj, proj, proj, bias, *cast)


def _attn_bias_rows(rpb):
    return jnp.pad(rpb[:, :, ::-1], ((0, 0), (0, 0), (0, LANES - rpb.shape[2])), constant_values=MASK_VALUE)


def _build_bias_table(rows_ref, bias_scr):
    shape = (GRID_W, LANES)
    kc = lax.broadcasted_iota(jnp.int32, shape, 0)
    lane = lax.broadcasted_iota(jnp.int32, shape, 1)
    second_head = lane >= GRID_W
    c = jnp.where(second_head, lane - GRID_W, lane)
    start = jnp.clip(c - WIN_COLS // 2, 0, GRID_W - WIN_COLS)
    inside = (kc >= start) & (kc < start + WIN_COLS)
    for d in range(bias_scr.shape[0]):
        t0 = pltpu.roll(jnp.broadcast_to(rows_ref[0, d:d + 1, :], shape), LANES - (WIN_COLS - 1), 1,
                        stride=1, stride_axis=0)
        t1 = pltpu.roll(jnp.broadcast_to(rows_ref[1, d:d + 1, :], shape), GRID_W - (WIN_COLS - 1), 1,
                        stride=1, stride_axis=0)
        bias_scr[d] = jnp.where(inside, jnp.where(second_head, t1, t0), MASK_VALUE)


def _out_proj_kernel(ya_ref, yb_ref, h_ref, ga_ref, gb_ref, wa_ref, wb_ref, o_ref):
    bsz, ts, d = h_ref.shape
    ya = jnp.swapaxes(ya_ref[...], 0, 1).reshape(bsz * ts, -1)
    yb = yb_ref[...].reshape(bsz * ts, -1)
    ua = _rms_norm(ya, ga_ref[...]).astype(jnp.bfloat16)
    ub = _rms_norm(yb, gb_ref[...]).astype(jnp.bfloat16)
    out = (h_ref[...].reshape(bsz * ts, d)
           + jnp.dot(ua, wa_ref[...], preferred_element_type=jnp.float32)
           + jnp.dot(ub, wb_ref[...], preferred_element_type=jnp.float32))
    o_ref[...] = out.reshape(bsz, ts, d)


def _out_proj(ya, yb, h, ga, gb, w):
    bsz, s, d = h.shape
    ca, cb = ya.shape[2], yb.shape[2]
    ts = OUT_TM // bsz
    return pl.pallas_call(
        _out_proj_kernel,
        grid=(s // ts, 1),
        in_specs=[
            pl.BlockSpec((ts, bsz, ca), lambda i, j: (i, 0, 0)),
            pl.BlockSpec((bsz, ts, cb), lambda i, j: (0, i, 0)),
            pl.BlockSpec((bsz, ts, d), lambda i, j: (0, i, 0)),
            pl.BlockSpec((1, ca), lambda i, j: (0, 0)),
            pl.BlockSpec((1, cb), lambda i, j: (0, 0)),
            pl.BlockSpec((ca, d), lambda i, j: (0, 0)),
            pl.BlockSpec((cb, d), lambda i, j: (1, 0)),
        ],
        out_specs=pl.BlockSpec((bsz, ts, d), lambda i, j: (0, i, 0)),
        out_shape=jax.ShapeDtypeStruct((bsz, s, d), jnp.float32),
        compiler_params=_vmem(48),
        name="out_proj",
    )(ya, yb, h, ga, gb, w, w)


def kernel(x, norm_ffn1, ffn1_w_in, ffn1_w_out, norm_mix, w_in_mix, lru_conv_w, lru_conv_b, lru_gate_w, lru_gate_b, lru_lambda, attn_rpb, lru_out_norm, attn_out_norm, w_out_mix, norm_ffn2, ffn2_w_in, ffn2_w_out, norm_final):
    bsz, s, d = x.shape
    depth = norm_ffn1.shape[0]
    assert depth >= 1 and d == D_MODEL
    bf16 = jnp.bfloat16
    t = bsz * s
    h = x.reshape(t, d)
    final_gain = norm_final.reshape(1, d)
    for l in range(depth):
        h, w_in_mix_bf, w_out_mix_bf = _ffn(
            h, norm_ffn1[l].reshape(1, d), ffn1_w_in[l].astype(bf16), ffn1_w_out[l].astype(bf16),
            final_gain, final_norm=False, cast=(w_in_mix[l], w_out_mix[l]))

        h3 = h.reshape(bsz, s, d)
        xg, qkv = _in_proj(h3, norm_mix[l].reshape(1, d), w_in_mix_bf)

        gate_w, gate_b = _lru_gate_weights(lru_gate_w[l], lru_gate_b[l])
        ya = _lru(xg, gate_w, gate_b, lru_conv_w[l], lru_conv_b[l].reshape(1, LRU_WIDTH),
                  lru_lambda[l].reshape(2, 1, LRU_WIDTH))
        yb, ffn2_w_in_bf, ffn2_w_out_bf = _attention(qkv, _attn_bias_rows(attn_rpb[l]),
                                                     cast=(ffn2_w_in[l], ffn2_w_out[l]))

        h = _out_proj(ya, yb, h3, lru_out_norm[l].reshape(1, LRU_WIDTH),
                      attn_out_norm[l].reshape(1, ATTN_WIDTH), w_out_mix_bf).reshape(t, d)

        last = l == depth - 1
        h, = _ffn(h, norm_ffn2[l].reshape(1, d), ffn2_w_in_bf, ffn2_w_out_bf, final_gain, final_norm=last)
    return h.reshape(bsz, s, d)
```

```python
import functools

import jax
import jax.numpy as jnp
from jax import lax
from jax.experimental import pallas as pl
from jax.experimental.pallas import tpu as pltpu

D_MODEL = 2048
HEAD_DIM = 64
LRU_WIDTH = 1024
ATTN_WIDTH = 1024
ATTN_HEADS = ATTN_WIDTH // HEAD_DIM
IN_PROJ_WIDTH = 2 * LRU_WIDTH + 3 * ATTN_WIDTH
CONV_WIDTH = 4
CONV_PAD_L = CONV_WIDTH // 2
RG_C = 8.0
GRID_W = 64
WIN_ROWS = 8
WIN_COLS = 16
D_FF = 5632
NORM_EPS = 1e-6

LANES = 128
SUBLANES = 8
BF16_SUBLANES = 16
MASK_VALUE = -1e30

FFN_TM = 1024
FFN_TF = 512
PROJ_TM = 1024
PROJ_TN = 1024
OUT_TM = 512
LRU_TB = 256
SCAN_UNROLL = 8
ATTN_SLOTS = 4


def _vmem(mib):
    return pltpu.CompilerParams(
        dimension_semantics=("arbitrary", "arbitrary"), vmem_limit_bytes=mib * 1024 * 1024)


def _rms_norm(x, gain):
    ms = jnp.mean(x * x, axis=-1, keepdims=True)
    return x * lax.rsqrt(ms + NORM_EPS) * gain


def _ffn_kernel(*refs, final_norm, n_cast):
    x_ref, g_ref, wg_ref, wu_ref, wo_ref, gf_ref = refs[:6]
    cast_src = refs[6:6 + n_cast]
    o_ref = refs[6 + n_cast]
    cast_dst = refs[7 + n_cast:7 + 2 * n_cast]
    u_ref = refs[7 + 2 * n_cast]
    k = pl.program_id(1)

    def step(first):
        _cast_blocks(cast_src, cast_dst)
        if first:
            u = _rms_norm(x_ref[...], g_ref[...]).astype(jnp.bfloat16)
            u_ref[...] = u
        else:
            u = u_ref[...]
        gate = jnp.dot(u, wg_ref[...], preferred_element_type=jnp.float32)
        up = jnp.dot(u, wu_ref[...], preferred_element_type=jnp.float32)
        act = (gate * jax.nn.sigmoid(gate)) * (0.5 * up)
        contrib = jnp.dot(act.astype(jnp.bfloat16), wo_ref[...], preferred_element_type=jnp.float32)
        o_ref[...] = (x_ref[...] if first else o_ref[...]) + contrib

    pl.when(k == 0)(functools.partial(step, True))
    pl.when(k > 0)(functools.partial(step, False))

    if final_norm:
        @pl.when(k == pl.num_programs(1) - 1)
        def _():
            o_ref[...] = _rms_norm(o_ref[...], gf_ref[...])


def _cast_specs(arrays, n_steps, flat_step):
    specs, shapes = [], []
    for w in arrays:
        rows = BF16_SUBLANES
        while w.shape[0] % rows or w.shape[0] // rows > n_steps:
            rows += BF16_SUBLANES
        specs.append(pl.BlockSpec(
            (rows, w.shape[1]), lambda *idx, nb=w.shape[0] // rows: (jnp.minimum(flat_step(*idx), nb - 1), 0)))
        shapes.append(jax.ShapeDtypeStruct(w.shape, jnp.bfloat16))
    return specs, shapes


def _cast_blocks(srcs, dsts):
    for src, dst in zip(srcs, dsts):
        dst[...] = src[...].astype(dst.dtype)


def _ffn(h, gain, w_in, w_out, final_gain, final_norm, cast=()):
    t, d = h.shape
    nk = D_FF // FFN_TF
    cast_specs, cast_shapes = _cast_specs(cast, (t // FFN_TM) * nk, lambda i, k: i * nk + k)
    return pl.pallas_call(
        functools.partial(_ffn_kernel, final_norm=final_norm, n_cast=len(cast)),
        grid=(t // FFN_TM, nk),
        in_specs=[
            pl.BlockSpec((FFN_TM, d), lambda i, k: (i, 0)),
            pl.BlockSpec((1, d), lambda i, k: (0, 0)),
            pl.BlockSpec((d, FFN_TF), lambda i, k: (0, k)),
            pl.BlockSpec((d, FFN_TF), lambda i, k: (0, k + nk)),
            pl.BlockSpec((FFN_TF, d), lambda i, k: (k, 0)),
            pl.BlockSpec((1, d), lambda i, k: (0, 0)),
        ] + cast_specs,
        out_specs=[pl.BlockSpec((FFN_TM, d), lambda i, k: (i, 0))] + cast_specs,
        out_shape=[jax.ShapeDtypeStruct((t, d), jnp.float32)] + cast_shapes,
        scratch_shapes=[pltpu.VMEM((FFN_TM, d), jnp.bfloat16)],
        compiler_params=_vmem(58),
        name="ffn_final" if final_norm else "ffn",
    )(h, gain, w_in, w_in, w_out, final_gain, *cast)


def _in_proj_kernel(x_ref, g_ref, w_ref, lru_ref, qkv_ref, u_ref, *, n_lru):
    n = pl.program_id(1)
    bsz, ts, d = x_ref.shape

    def step(first, time_major):
        if first:
            u = _rms_norm(x_ref[...].reshape(bsz * ts, d), g_ref[...]).astype(jnp.bfloat16)
            u_ref[...] = u
        else:
            u = u_ref[...]
        res = jnp.dot(u, w_ref[...], preferred_element_type=jnp.float32).reshape(bsz, ts, -1)
        if time_major:
            lru_ref[...] = jnp.swapaxes(res, 0, 1)
        else:
            qkv_ref[...] = res.astype(qkv_ref.dtype)

    pl.when(n == 0)(functools.partial(step, True, True))
    pl.when((n > 0) & (n < n_lru))(functools.partial(step, False, True))
    pl.when(n >= n_lru)(functools.partial(step, False, False))


def _in_proj(h, gain, w):
    bsz, s, d = h.shape
    ts = PROJ_TM // bsz
    n_lru = 2 * LRU_WIDTH // PROJ_TN
    n_qkv = 3 * ATTN_WIDTH // PROJ_TN
    return pl.pallas_call(
        functools.partial(_in_proj_kernel, n_lru=n_lru),
        grid=(s // ts, n_lru + n_qkv),
        in_specs=[
            pl.BlockSpec((bsz, ts, d), lambda i, n: (0, i, 0)),
            pl.BlockSpec((1, d), lambda i, n: (0, 0)),
            pl.BlockSpec((d, PROJ_TN), lambda i, n: (0, n)),
        ],
        out_specs=[
            pl.BlockSpec((ts, bsz, PROJ_TN), lambda i, n: (i, 0, jnp.minimum(n, n_lru - 1))),
            pl.BlockSpec((bsz, ts, PROJ_TN), lambda i, n: (0, i, jnp.maximum(n - n_lru, 0))),
        ],
        out_shape=[
            jax.ShapeDtypeStruct((s, bsz, 2 * LRU_WIDTH), jnp.float32),
            jax.ShapeDtypeStruct((bsz, s, 3 * ATTN_WIDTH), jnp.bfloat16),
        ],
        scratch_shapes=[pltpu.VMEM((PROJ_TM, d), jnp.bfloat16)],
        compiler_params=_vmem(56),
        name="in_proj",
    )(h, gain, w)


def _lru_kernel(x_ref, gf_ref, gr_ref, w_ref, gb_ref, cw_ref, cb_ref, lam_ref, o_ref,
                xw_scr, a_scr, b_scr, hs_scr, h_scr, *, n_chunks):
    j = pl.program_id(1)
    seq = x_ref.shape[0]
    rows = LRU_TB * SUBLANES
    n_tail = CONV_WIDTH - 1 - CONV_PAD_L
    second_visit = j >= n_chunks // 2

    @pl.when(j == 0)
    def _():
        h_scr[...] = jnp.zeros_like(h_scr)

    starts = []
    for d in range(2):
        chunk = (n_chunks - 1 - j) if d else j
        t0 = pl.multiple_of(chunk * LRU_TB, LRU_TB)
        starts.append(t0)
        xw_scr[d, CONV_PAD_L:CONV_PAD_L + LRU_TB] = x_ref[pl.ds(t0, LRU_TB)]
        head = x_ref[pl.ds(jnp.maximum(t0 - CONV_PAD_L, 0), CONV_PAD_L)]
        xw_scr[d, :CONV_PAD_L] = jnp.where(chunk > 0, head, 0.0)
        tail = x_ref[pl.ds(jnp.minimum(t0 + LRU_TB, seq - n_tail), n_tail)]
        xw_scr[d, CONV_PAD_L + LRU_TB:CONV_PAD_L + LRU_TB + n_tail] = jnp.where(chunk < n_chunks - 1, tail, 0.0)

        xh = cw_ref[0:1, :][None] * xw_scr[d, 0:LRU_TB]
        for tap in range(1, CONV_WIDTH):
            xh = xh + cw_ref[tap:tap + 1, :][None] * xw_scr[d, tap:tap + LRU_TB]
        xh2 = xh.reshape(rows, LANES) + cb_ref[...]
        t = jnp.tanh(jnp.dot(xh2.astype(jnp.bfloat16), w_ref[d, 0], preferred_element_type=jnp.float32)
                     + gb_ref[d, 0])
        t_r, t_i = t[:, :LANES], t[:, LANES:]
        z = -lam_ref[d]
        softplus = jnp.maximum(z, 0.0) + jnp.log1p(jnp.exp(-jnp.abs(z)))
        half_c = (-0.5 * RG_C) * softplus
        log_a = half_c + half_c * t_r
        a = jnp.exp(log_a)
        sq = -jnp.tanh(log_a) * (a * a + 1.0)
        root = jnp.where(sq > 0.0, sq * lax.rsqrt(sq), 0.0)
        b = (root * xh2) * (1.0 + t_i)
        a_scr[d] = a.reshape(LRU_TB, SUBLANES, LANES)
        b_scr[d] = b.reshape(LRU_TB, SUBLANES, LANES)

    def step(s, carry):
        hf, hr = carry
        tr = LRU_TB - 1 - s
        hf = a_scr[0, s] * hf + b_scr[0, s]
        hr = a_scr[1, tr] * hr + b_scr[1, tr]
        hs_scr[0, s] = hf
        hs_scr[1, tr] = hr
        return hf, hr

    hf, hr = lax.fori_loop(0, LRU_TB, step, (h_scr[0], h_scr[1]), unroll=SCAN_UNROLL)
    h_scr[0] = hf
    h_scr[1] = hr

    for d, g_ref in enumerate((gf_ref, gr_ref)):
        window = pl.ds(starts[d], LRU_TB)

        @pl.when(jnp.logical_not(second_visit))
        def _():
            o_ref[window] = hs_scr[d]

        @pl.when(second_visit)
        def _():
            o_ref[window] = jax.nn.gelu(g_ref[...], approximate=True) * (o_ref[window] + hs_scr[d])


def _lru(xg, gate_w, gate_b, conv_w, conv_b, lam):
    s, bsz, c2 = xg.shape
    c = c2 // 2
    n_cblk = c // LANES
    n_chunks = s // LRU_TB
    half = n_chunks // 2
    assert n_chunks % 2 == 0
    gate_fwd = pl.BlockSpec((LRU_TB, bsz, LANES), lambda cb, j: (jnp.maximum(j, half), 0, n_cblk + cb))
    gate_rev = pl.BlockSpec((LRU_TB, bsz, LANES),
                            lambda cb, j: (jnp.minimum(n_chunks - 1 - j, half - 1), 0, n_cblk + cb))
    return pl.pallas_call(
        functools.partial(_lru_kernel, n_chunks=n_chunks),
        grid=(n_cblk, n_chunks),
        in_specs=[
            pl.BlockSpec((s, bsz, LANES), lambda cb, j: (0, 0, cb)),
            gate_fwd,
            gate_rev,
            pl.BlockSpec((2, 1, LANES, 2 * LANES), lambda cb, j: (0, cb, 0, 0)),
            pl.BlockSpec((2, 1, 1, 2 * LANES), lambda cb, j: (0, cb, 0, 0)),
            pl.BlockSpec((CONV_WIDTH, LANES), lambda cb, j: (0, cb)),
            pl.BlockSpec((1, LANES), lambda cb, j: (0, cb)),
            pl.BlockSpec((2, 1, LANES), lambda cb, j: (0, 0, cb)),
        ],
        out_specs=pl.BlockSpec((s, bsz, LANES), lambda cb, j: (0, 0, cb)),
        out_shape=jax.ShapeDtypeStruct((s, bsz, c), jnp.float32),
        scratch_shapes=[
            pltpu.VMEM((2, LRU_TB + SUBLANES, bsz, LANES), jnp.float32),
            pltpu.VMEM((2, LRU_TB, bsz, LANES), jnp.float32),
            pltpu.VMEM((2, LRU_TB, bsz, LANES), jnp.float32),
            pltpu.VMEM((2, LRU_TB, bsz, LANES), jnp.float32),
            pltpu.VMEM((2, bsz, LANES), jnp.float32),
        ],
        compiler_params=_vmem(56),
        name="lru",
    )(xg, xg, xg, gate_w, 0.5 * gate_b, 0.5 * conv_w, 0.5 * conv_b, lam)


def _lru_gate_weights(gate_w, gate_b):
    n_cblk = LRU_WIDTH // LANES
    hpb = LANES // HEAD_DIM
    w = gate_w.reshape(2, 2, n_cblk, hpb, HEAD_DIM, HEAD_DIM)
    eye = jnp.eye(hpb, dtype=gate_w.dtype)
    wbd = w[:, :, :, :, :, None, :] * eye[None, None, None, :, None, :, None]
    wbd = wbd.reshape(2, 2, n_cblk, LANES, LANES)
    wbd = wbd.transpose(0, 2, 3, 1, 4).reshape(2, n_cblk, LANES, 2 * LANES)
    bias = gate_b.reshape(2, 2, n_cblk, LANES).transpose(0, 2, 1, 3).reshape(2, n_cblk, 1, 2 * LANES)
    return wbd.astype(jnp.bfloat16), bias


def _attn_kernel(*refs, rows, kr, n_cast):
    q_ref, k_ref, v_ref, rows_ref = refs[:4]
    o_ref = refs[4 + n_cast]
    q_scr, v_scr, p_scr, bias_scr = refs[5 + 2 * n_cast:]
    _cast_blocks(refs[4:4 + n_cast], refs[5 + n_cast:5 + 2 * n_cast])

    @pl.when(pl.program_id(1) == 0)
    def _():
        _build_bias_table(rows_ref, bias_scr)

    lane = lax.broadcasted_iota(jnp.int32, (1, 1, LANES), 2)
    first_head = lane < HEAD_DIM
    q = q_ref[0].reshape(rows, GRID_W, LANES) * HEAD_DIM ** -0.5
    for r in range(rows):
        stacked = jnp.concatenate([jnp.where(first_head[0], q[r], 0.0), jnp.where(first_head[0], 0.0, q[r])], axis=0)
        q_scr[r] = stacked.T
    v_scr[:, :LANES] = v_ref[0]
    v_scr[:, LANES:] = jnp.ones((rows * GRID_W, LANES), jnp.bfloat16)
    band = kr * GRID_W

    def band_start(r):
        return min(max(r - kr // 2, 0), rows - kr)

    def scores(r):
        rs = band_start(r)
        s_t = jnp.dot(k_ref[0, rs * GRID_W:rs * GRID_W + band, :], q_scr[r],
                      preferred_element_type=jnp.float32)
        s_t = s_t + jnp.concatenate([bias_scr[rs + i - r + WIN_ROWS - 1] for i in range(kr)], axis=0)
        return s_t

    def exponentials(r, s_t):
        m = jnp.max(s_t, axis=0, keepdims=True)
        p_scr[r % ATTN_SLOTS] = jnp.exp(s_t - m).astype(jnp.bfloat16)

    def weighted_values(r):
        rs = band_start(r)
        pv = lax.dot_general(p_scr[r % ATTN_SLOTS], v_scr[rs * GRID_W:rs * GRID_W + band, :],
                             (((0,), (0,)), ((), ())), preferred_element_type=jnp.float32)
        out = pv[:, :LANES] / pv[:, LANES:]
        o_ref[0, r * GRID_W:(r + 1) * GRID_W, :] = jnp.where(first_head[0], out[:GRID_W], out[GRID_W:])

    for step in range(rows + 2):
        if step >= 2:
            weighted_values(step - 2)
        if 1 <= step <= rows:
            exponentials(step - 1, pending)
        if step < rows:
            pending = scores(step)


def _attention(proj, bias, cast=()):
    bsz, s, _ = proj.shape
    rows = s // GRID_W
    kr = min(WIN_ROWS, rows)
    n_hp = ATTN_WIDTH // LANES
    q_blk = 0
    k_blk = q_blk + n_hp
    v_blk = k_blk + n_hp
    blk = (1, s, LANES)
    cast_specs, cast_shapes = _cast_specs(cast, n_hp * bsz, lambda hp, b: hp * bsz + b)
    return pl.pallas_call(
        functools.partial(_attn_kernel, rows=rows, kr=kr, n_cast=len(cast)),
        grid=(n_hp, bsz),
        in_specs=[
            pl.BlockSpec(blk, lambda hp, b: (b, 0, q_blk + hp)),
            pl.BlockSpec(blk, lambda hp, b: (b, 0, k_blk + hp)),
            pl.BlockSpec(blk, lambda hp, b: (b, 0, v_blk + hp)),
            pl.BlockSpec((2, 2 * WIN_ROWS - 1, LANES), lambda hp, b: (hp, 0, 0)),
        ] + cast_specs,
        out_specs=[pl.BlockSpec(blk, lambda hp, b: (b, 0, hp))] + cast_specs,
        out_shape=[jax.ShapeDtypeStruct((bsz, s, ATTN_WIDTH), jnp.float32)] + cast_shapes,
        scratch_shapes=[
            pltpu.VMEM((rows, 2 * GRID_W, LANES), jnp.bfloat16),
            pltpu.VMEM((s, 2 * LANES), jnp.bfloat16),
            pltpu.VMEM((ATTN_SLOTS, kr * GRID_W, LANES), jnp.bfloat16),
            pltpu.VMEM((2 * WIN_ROWS - 1, GRID_W, LANES), jnp.float32),
        ],
        compiler_params=_vmem(32),
        name="attn",
    )(proj, proj, proj, bias, *cast)


def _attn_bias_rows(rpb):
    return jnp.pad(rpb[:, :, ::-1], ((0, 0), (0, 0), (0, LANES - rpb.shape[2])), constant_values=MASK_VALUE)


def _build_bias_table(rows_ref, bias_scr):
    shape = (GRID_W, LANES)
    kc = lax.broadcasted_iota(jnp.int32, shape, 0)
    lane = lax.broadcasted_iota(jnp.int32, shape, 1)
    second_head = lane >= GRID_W
    c = jnp.where(second_head, lane - GRID_W, lane)
    start = jnp.clip(c - WIN_COLS // 2, 0, GRID_W - WIN_COLS)
    inside = (kc >= start) & (kc < start + WIN_COLS)
    for d in range(bias_scr.shape[0]):
        t0 = pltpu.roll(jnp.broadcast_to(rows_ref[0, d:d + 1, :], shape), LANES - (WIN_COLS - 1), 1,
                        stride=1, stride_axis=0)
        t1 = pltpu.roll(jnp.broadcast_to(rows_ref[1, d:d + 1, :], shape), GRID_W - (WIN_COLS - 1), 1,
                        stride=1, stride_axis=0)
        bias_scr[d] = jnp.where(inside, jnp.where(second_head, t1, t0), MASK_VALUE)


def _out_proj_kernel(ya_ref, yb_ref, h_ref, ga_ref, gb_ref, wa_ref, wb_ref, o_ref):
    bsz, ts, d = h_ref.shape
    ya = jnp.swapaxes(ya_ref[...], 0, 1).reshape(bsz * ts, -1)
    yb = yb_ref[...].reshape(bsz * ts, -1)
    ua = _rms_norm(ya, ga_ref[...]).astype(jnp.bfloat16)
    ub = _rms_norm(yb, gb_ref[...]).astype(jnp.bfloat16)
    out = (h_ref[...].reshape(bsz * ts, d)
           + jnp.dot(ua, wa_ref[...], preferred_element_type=jnp.float32)
           + jnp.dot(ub, wb_ref[...], preferred_element_type=jnp.float32))
    o_ref[...] = out.reshape(bsz, ts, d)


def _out_proj(ya, yb, h, ga, gb, w):
    bsz, s, d = h.shape
    ca, cb = ya.shape[2], yb.shape[2]
    ts = OUT_TM // bsz
    return pl.pallas_call(
        _out_proj_kernel,
        grid=(s // ts, 1),
        in_specs=[
            pl.BlockSpec((ts, bsz, ca), lambda i, j: (i, 0, 0)),
            pl.BlockSpec((bsz, ts, cb), lambda i, j: (0, i, 0)),
            pl.BlockSpec((bsz, ts, d), lambda i, j: (0, i, 0)),
            pl.BlockSpec((1, ca), lambda i, j: (0, 0)),
            pl.BlockSpec((1, cb), lambda i, j: (0, 0)),
            pl.BlockSpec((ca, d), lambda i, j: (0, 0)),
            pl.BlockSpec((cb, d), lambda i, j: (1, 0)),
        ],
        out_specs=pl.BlockSpec((bsz, ts, d), lambda i, j: (0, i, 0)),
        out_shape=jax.ShapeDtypeStruct((bsz, s, d), jnp.float32),
        compiler_params=_vmem(48),
        name="out_proj",
    )(ya, yb, h, ga, gb, w, w)


def kernel(x, norm_ffn1, ffn1_w_in, ffn1_w_out, norm_mix, w_in_mix, lru_conv_w, lru_conv_b, lru_gate_w, lru_gate_b, lru_lambda, attn_rpb, lru_out_norm, attn_out_norm, w_out_mix, norm_ffn2, ffn2_w_in, ffn2_w_out, norm_final):
    bsz, s, d = x.shape
    depth = norm_ffn1.shape[0]
    assert depth >= 1 and d == D_MODEL
    bf16 = jnp.bfloat16
    t = bsz * s
    h = x.reshape(t, d)
    final_gain = norm_final.reshape(1, d)
    for l in range(depth):
        h, w_in_mix_bf, w_out_mix_bf = _ffn(
            h, norm_ffn1[l].reshape(1, d), ffn1_w_in[l].astype(bf16), ffn1_w_out[l].astype(bf16),
            final_gain, final_norm=False, cast=(w_in_mix[l], w_out_mix[l]))

        h3 = h.reshape(bsz, s, d)
        xg, qkv = _in_proj(h3, norm_mix[l].reshape(1, d), w_in_mix_bf)

        gate_w, gate_b = _lru_gate_weights(lru_gate_w[l], lru_gate_b[l])
        ya = _lru(xg, gate_w, gate_b, lru_conv_w[l], lru_conv_b[l].reshape(1, LRU_WIDTH),
                  lru_lambda[l].reshape(2, 1, LRU_WIDTH))
        yb, ffn2_w_in_bf, ffn2_w_out_bf = _attention(qkv, _attn_bias_rows(attn_rpb[l]),
                                                     cast=(ffn2_w_in[l], ffn2_w_out[l]))

        h = _out_proj(ya, yb, h3, lru_out_norm[l].reshape(1, LRU_WIDTH),
                      attn_out_norm[l].reshape(1, ATTN_WIDTH), w_out_mix_bf).reshape(t, d)

        last = l == depth - 1
        h, = _ffn(h, norm_ffn2[l].reshape(1, d), ffn2_w_in_bf, ffn2_w_out_bf, final_gain, final_norm=last)
    return h.reshape(bsz, s, d)
```

```python
import functools

import jax
import jax.numpy as jnp
from jax import lax
from jax.experimental import pallas as pl
from jax.experimental.pallas import tpu as pltpu

D_MODEL = 2048
HEAD_DIM = 64
LRU_WIDTH = 1024
ATTN_WIDTH = 1024
CONV_WIDTH = 4
CONV_PAD_L = CONV_WIDTH // 2
RG_C = 8.0
GRID_W = 64
WIN_ROWS = 8
WIN_COLS = 16
D_FF = 5632
NORM_EPS = 1e-6

LANES = 128
SUBLANES = 8
BF16_SUBLANES = 16
MASK_VALUE = -1e30

FFN_TM = 1024
FFN_TF = 512
PROJ_TM = 1024
PROJ_TN = 1024
OUT_TM = 512
LRU_TB = 256
SCAN_UNROLL = 32
ATTN_SLOTS = 4


def _vmem(mib):
    return pltpu.CompilerParams(
        dimension_semantics=("arbitrary", "arbitrary"), vmem_limit_bytes=mib * 1024 * 1024)


def _rms_norm(x, gain):
    ms = jnp.mean(x * x, axis=-1, keepdims=True)
    return x * lax.rsqrt(ms + NORM_EPS) * gain


def _ffn_kernel(*refs, final_norm, n_cast):
    x_ref, g_ref, wg_ref, wu_ref, wo_ref, gf_ref = refs[:6]
    cast_src = refs[6:6 + n_cast]
    o_ref = refs[6 + n_cast]
    cast_dst = refs[7 + n_cast:7 + 2 * n_cast]
    u_ref = refs[7 + 2 * n_cast]
    k = pl.program_id(1)

    def step(first):
        _cast_blocks(cast_src, cast_dst)
        if first:
            u = _rms_norm(x_ref[...], g_ref[...]).astype(jnp.bfloat16)
            u_ref[...] = u
        else:
            u = u_ref[...]
        gate = jnp.dot(u, wg_ref[...], preferred_element_type=jnp.float32)
        up = jnp.dot(u, wu_ref[...], preferred_element_type=jnp.float32)
        act = (gate * jax.nn.sigmoid(gate)) * (0.5 * up)
        contrib = jnp.dot(act.astype(jnp.bfloat16), wo_ref[...], preferred_element_type=jnp.float32)
        o_ref[...] = (x_ref[...] if first else o_ref[...]) + contrib

    pl.when(k == 0)(functools.partial(step, True))
    pl.when(k > 0)(functools.partial(step, False))

    if final_norm:
        @pl.when(k == pl.num_programs(1) - 1)
        def _():
            o_ref[...] = _rms_norm(o_ref[...], gf_ref[...])


def _cast_specs(arrays, n_steps, flat_step):
    specs, shapes = [], []
    for w in arrays:
        rows = BF16_SUBLANES
        while w.shape[0] % rows or w.shape[0] // rows > n_steps:
            rows += BF16_SUBLANES
        specs.append(pl.BlockSpec(
            (rows, w.shape[1]), lambda *idx, nb=w.shape[0] // rows: (jnp.minimum(flat_step(*idx), nb - 1), 0)))
        shapes.append(jax.ShapeDtypeStruct(w.shape, jnp.bfloat16))
    return specs, shapes


def _cast_blocks(srcs, dsts):
    for src, dst in zip(srcs, dsts):
        dst[...] = src[...].astype(dst.dtype)


def _ffn(h, gain, w_in, w_out, final_gain, final_norm, cast=()):
    t, d = h.shape
    nk = D_FF // FFN_TF
    cast_specs, cast_shapes = _cast_specs(cast, (t // FFN_TM) * nk, lambda i, k: i * nk + k)
    return pl.pallas_call(
        functools.partial(_ffn_kernel, final_norm=final_norm, n_cast=len(cast)),
        grid=(t // FFN_TM, nk),
        in_specs=[
            pl.BlockSpec((FFN_TM, d), lambda i, k: (i, 0)),
            pl.BlockSpec((1, d), lambda i, k: (0, 0)),
            pl.BlockSpec((d, FFN_TF), lambda i, k: (0, k)),
            pl.BlockSpec((d, FFN_TF), lambda i, k: (0, k + nk)),
            pl.BlockSpec((FFN_TF, d), lambda i, k: (k, 0)),
            pl.BlockSpec((1, d), lambda i, k: (0, 0)),
        ] + cast_specs,
        out_specs=[pl.BlockSpec((FFN_TM, d), lambda i, k: (i, 0))] + cast_specs,
        out_shape=[jax.ShapeDtypeStruct((t, d), jnp.float32)] + cast_shapes,
        scratch_shapes=[pltpu.VMEM((FFN_TM, d), jnp.bfloat16)],
        compiler_params=_vmem(58),
        name="ffn_final" if final_norm else "ffn",
    )(h, gain, w_in, w_in, w_out, final_gain, *cast)


def _in_proj_kernel(x_ref, g_ref, w_ref, lru_ref, qkv_ref, u_ref, *, n_lru):
    n = pl.program_id(1)
    bsz, ts, d = x_ref.shape

    def step(first, time_major):
        if first:
            u = _rms_norm(x_ref[...].reshape(bsz * ts, d), g_ref[...]).astype(jnp.bfloat16)
            u_ref[...] = u
        else:
            u = u_ref[...]
        res = jnp.dot(u, w_ref[...], preferred_element_type=jnp.float32).reshape(bsz, ts, -1)
        if time_major:
            lru_ref[...] = jnp.swapaxes(res, 0, 1)
        else:
            qkv_ref[...] = res.astype(qkv_ref.dtype)

    pl.when(n == 0)(functools.partial(step, True, True))
    pl.when((n > 0) & (n < n_lru))(functools.partial(step, False, True))
    pl.when(n >= n_lru)(functools.partial(step, False, False))


def _in_proj(h, gain, w):
    bsz, s, d = h.shape
    ts = PROJ_TM // bsz
    n_lru = 2 * LRU_WIDTH // PROJ_TN
    n_qkv = 3 * ATTN_WIDTH // PROJ_TN
    return pl.pallas_call(
        functools.partial(_in_proj_kernel, n_lru=n_lru),
        grid=(s // ts, n_lru + n_qkv),
        in_specs=[
            pl.BlockSpec((bsz, ts, d), lambda i, n: (0, i, 0)),
            pl.BlockSpec((1, d), lambda i, n: (0, 0)),
            pl.BlockSpec((d, PROJ_TN), lambda i, n: (0, n)),
        ],
        out_specs=[
            pl.BlockSpec((ts, bsz, PROJ_TN), lambda i, n: (i, 0, jnp.minimum(n, n_lru - 1))),
            pl.BlockSpec((bsz, ts, PROJ_TN), lambda i, n: (0, i, jnp.maximum(n - n_lru, 0))),
        ],
        out_shape=[
            jax.ShapeDtypeStruct((s, bsz, 2 * LRU_WIDTH), jnp.float32),
            jax.ShapeDtypeStruct((bsz, s, 3 * ATTN_WIDTH), jnp.bfloat16),
        ],
        scratch_shapes=[pltpu.VMEM((PROJ_TM, d), jnp.bfloat16)],
        compiler_params=_vmem(56),
        name="in_proj",
    )(h, gain, w)


def _lru_kernel(x_ref, gf_ref, gr_ref, w_ref, gb_ref, cw_ref, cb_ref, lam_ref, o_ref,
                xw_scr, a_scr, b_scr, hs_scr, h_scr, *, n_chunks):
    j = pl.program_id(1)
    seq = x_ref.shape[0]
    rows = LRU_TB * SUBLANES
    n_tail = CONV_WIDTH - 1 - CONV_PAD_L
    second_visit = j >= n_chunks // 2

    @pl.when(j == 0)
    def _():
        h_scr[...] = jnp.zeros_like(h_scr)

    starts = []
    for d in range(2):
        chunk = (n_chunks - 1 - j) if d else j
        t0 = pl.multiple_of(chunk * LRU_TB, LRU_TB)
        starts.append(t0)
        xw_scr[d, CONV_PAD_L:CONV_PAD_L + LRU_TB] = x_ref[pl.ds(t0, LRU_TB)]
        head = x_ref[pl.ds(jnp.maximum(t0 - CONV_PAD_L, 0), CONV_PAD_L)]
        xw_scr[d, :CONV_PAD_L] = jnp.where(chunk > 0, head, 0.0)
        tail = x_ref[pl.ds(jnp.minimum(t0 + LRU_TB, seq - n_tail), n_tail)]
        xw_scr[d, CONV_PAD_L + LRU_TB:CONV_PAD_L + LRU_TB + n_tail] = jnp.where(chunk < n_chunks - 1, tail, 0.0)

        xh = cw_ref[0:1, :][None] * xw_scr[d, 0:LRU_TB]
        for tap in range(1, CONV_WIDTH):
            xh = xh + cw_ref[tap:tap + 1, :][None] * xw_scr[d, tap:tap + LRU_TB]
        xh2 = xh.reshape(rows, LANES) + cb_ref[...]
        t = jnp.tanh(jnp.dot(xh2.astype(jnp.bfloat16), w_ref[d, 0], preferred_element_type=jnp.float32)
                     + gb_ref[d, 0])
        t_r, t_i = t[:, :LANES], t[:, LANES:]
        z = -lam_ref[d]
        softplus = jnp.maximum(z, 0.0) + jnp.log1p(jnp.exp(-jnp.abs(z)))
        half_c = (-0.5 * RG_C) * softplus
        log_a = half_c + half_c * t_r
        a = jnp.exp(log_a)
        sq = -jnp.tanh(log_a) * (a * a + 1.0)
        root = jnp.where(sq > 0.0, sq * lax.rsqrt(sq), 0.0)
        b = (root * xh2) * (1.0 + t_i)
        a_scr[d] = a.reshape(LRU_TB, SUBLANES, LANES)
        b_scr[d] = b.reshape(LRU_TB, SUBLANES, LANES)

    def step(s, carry):
        hf, hr = carry
        tr = LRU_TB - 1 - s
        hf = a_scr[0, s] * hf + b_scr[0, s]
        hr = a_scr[1, tr] * hr + b_scr[1, tr]
        hs_scr[0, s] = hf
        hs_scr[1, tr] = hr
        return hf, hr

    hf, hr = lax.fori_loop(0, LRU_TB, step, (h_scr[0], h_scr[1]), unroll=SCAN_UNROLL)
    h_scr[0] = hf
    h_scr[1] = hr

    for d, g_ref in enumerate((gf_ref, gr_ref)):
        window = pl.ds(starts[d], LRU_TB)

        @pl.when(jnp.logical_not(second_visit))
        def _():
            o_ref[window] = hs_scr[d]

        @pl.when(second_visit)
        def _():
            o_ref[window] = jax.nn.gelu(g_ref[...], approximate=True) * (o_ref[window] + hs_scr[d])


def _lru(xg, gate_w, gate_b, conv_w, conv_b, lam):
    s, bsz, c2 = xg.shape
    c = c2 // 2
    n_cblk = c // LANES
    n_chunks = s // LRU_TB
    half = n_chunks // 2
    assert n_chunks % 2 == 0
    gate_fwd = pl.BlockSpec((LRU_TB, bsz, LANES), lambda cb, j: (jnp.maximum(j, half), 0, n_cblk + cb))
    gate_rev = pl.BlockSpec((LRU_TB, bsz, LANES),
                            lambda cb, j: (jnp.minimum(n_chunks - 1 - j, half - 1), 0, n_cblk + cb))
    return pl.pallas_call(
        functools.partial(_lru_kernel, n_chunks=n_chunks),
        grid=(n_cblk, n_chunks),
        in_specs=[
            pl.BlockSpec((s, bsz, LANES), lambda cb, j: (0, 0, cb)),
            gate_fwd,
            gate_rev,
            pl.BlockSpec((2, 1, LANES, 2 * LANES), lambda cb, j: (0, cb, 0, 0)),
            pl.BlockSpec((2, 1, 1, 2 * LANES), lambda cb, j: (0, cb, 0, 0)),
            pl.BlockSpec((CONV_WIDTH, LANES), lambda cb, j: (0, cb)),
            pl.BlockSpec((1, LANES), lambda cb, j: (0, cb)),
            pl.BlockSpec((2, 1, LANES), lambda cb, j: (0, 0, cb)),
        ],
        out_specs=pl.BlockSpec((s, bsz, LANES), lambda cb, j: (0, 0, cb)),
        out_shape=jax.ShapeDtypeStruct((s, bsz, c), jnp.float32),
        scratch_shapes=[
            pltpu.VMEM((2, LRU_TB + SUBLANES, bsz, LANES), jnp.float32),
            pltpu.VMEM((2, LRU_TB, bsz, LANES), jnp.float32),
            pltpu.VMEM((2, LRU_TB, bsz, LANES), jnp.float32),
            pltpu.VMEM((2, LRU_TB, bsz, LANES), jnp.float32),
            pltpu.VMEM((2, bsz, LANES), jnp.float32),
        ],
        compiler_params=_vmem(56),
        name="lru",
    )(xg, xg, xg, gate_w, 0.5 * gate_b, 0.5 * conv_w, 0.5 * conv_b, lam)


def _lru_gate_weights(gate_w, gate_b):
    n_cblk = LRU_WIDTH // LANES
    hpb = LANES // HEAD_DIM
    w = gate_w.reshape(2, 2, n_cblk, hpb, HEAD_DIM, HEAD_DIM)
    eye = jnp.eye(hpb, dtype=gate_w.dtype)
    wbd = w[:, :, :, :, :, None, :] * eye[None, None, None, :, None, :, None]
    wbd = wbd.reshape(2, 2, n_cblk, LANES, LANES)
    wbd = wbd.transpose(0, 2, 3, 1, 4).reshape(2, n_cblk, LANES, 2 * LANES)
    bias = gate_b.reshape(2, 2, n_cblk, LANES).transpose(0, 2, 1, 3).reshape(2, n_cblk, 1, 2 * LANES)
    return wbd.astype(jnp.bfloat16), bias


def _attn_kernel(*refs, rows, kr, n_cast):
    q_ref, k_ref, v_ref, rows_ref = refs[:4]
    o_ref = refs[4 + n_cast]
    q_scr, v_scr, p_scr, bias_scr = refs[5 + 2 * n_cast:]
    _cast_blocks(refs[4:4 + n_cast], refs[5 + n_cast:5 + 2 * n_cast])

    @pl.when(pl.program_id(1) == 0)
    def _():
        _build_bias_table(rows_ref, bias_scr)

    lane = lax.broadcasted_iota(jnp.int32, (1, 1, LANES), 2)
    first_head = lane < HEAD_DIM
    q = q_ref[0].reshape(rows, GRID_W, LANES) * HEAD_DIM ** -0.5
    for r in range(rows):
        stacked = jnp.concatenate([jnp.where(first_head[0], q[r], 0.0), jnp.where(first_head[0], 0.0, q[r])], axis=0)
        q_scr[r] = stacked.T
    v_scr[:, :LANES] = v_ref[0]
    v_scr[:, LANES:] = jnp.ones((rows * GRID_W, LANES), jnp.bfloat16)
    band = kr * GRID_W

    def band_start(r):
        return min(max(r - kr // 2, 0), rows - kr)

    def scores(r):
        rs = band_start(r)
        s_t = jnp.dot(k_ref[0, rs * GRID_W:rs * GRID_W + band, :], q_scr[r],
                      preferred_element_type=jnp.float32)
        s_t = s_t + jnp.concatenate([bias_scr[rs + i - r + WIN_ROWS - 1] for i in range(kr)], axis=0)
        return s_t

    def exponentials(r, s_t):
        m = jnp.max(s_t, axis=0, keepdims=True)
        p_scr[r % ATTN_SLOTS] = jnp.exp(s_t - m).astype(jnp.bfloat16)

    def weighted_values(r):
        rs = band_start(r)
        pv = lax.dot_general(p_scr[r % ATTN_SLOTS], v_scr[rs * GRID_W:rs * GRID_W + band, :],
                             (((0,), (0,)), ((), ())), preferred_element_type=jnp.float32)
        out = pv[:, :LANES] / pv[:, LANES:]
        o_ref[0, r * GRID_W:(r + 1) * GRID_W, :] = jnp.where(first_head[0], out[:GRID_W], out[GRID_W:])

    for step in range(rows + 2):
        if step >= 2:
            weighted_values(step - 2)
        if 1 <= step <= rows:
            exponentials(step - 1, pending)
        if step < rows:
            pending = scores(step)


def _attention(proj, bias, cast=()):
    bsz, s, _ = proj.shape
    rows = s // GRID_W
    kr = min(WIN_ROWS, rows)
    n_hp = ATTN_WIDTH // LANES
    q_blk = 0
    k_blk = q_blk + n_hp
    v_blk = k_blk + n_hp
    blk = (1, s, LANES)
    cast_specs, cast_shapes = _cast_specs(cast, n_hp * bsz, lambda hp, b: hp * bsz + b)
    return pl.pallas_call(
        functools.partial(_attn_kernel, rows=rows, kr=kr, n_cast=len(cast)),
        grid=(n_hp, bsz),
        in_specs=[
            pl.BlockSpec(blk, lambda hp, b: (b, 0, q_blk + hp)),
            pl.BlockSpec(blk, lambda hp, b: (b, 0, k_blk + hp)),
            pl.BlockSpec(blk, lambda hp, b: (b, 0, v_blk + hp)),
            pl.BlockSpec((2, 2 * WIN_ROWS - 1, LANES), lambda hp, b: (hp, 0, 0)),
        ] + cast_specs,
        out_specs=[pl.BlockSpec(blk, lambda hp, b: (b, 0, hp))] + cast_specs,
        out_shape=[jax.ShapeDtypeStruct((bsz, s, ATTN_WIDTH), jnp.float32)] + cast_shapes,
        scratch_shapes=[
            pltpu.VMEM((rows, 2 * GRID_W, LANES), jnp.bfloat16),
            pltpu.VMEM((s, 2 * LANES), jnp.bfloat16),
            pltpu.VMEM((ATTN_SLOTS, kr * GRID_W, LANES), jnp.bfloat16),
            pltpu.VMEM((2 * WIN_ROWS - 1, GRID_W, LANES), jnp.float32),
        ],
        compiler_params=_vmem(32),
        name="attn",
    )(proj, proj, proj, bias, *cast)


def _attn_bias_rows(rpb):
    return jnp.pad(rpb[:, :, ::-1], ((0, 0), (0, 0), (0, LANES - rpb.shape[2])), constant_values=MASK_VALUE)


def _build_bias_table(rows_ref, bias_scr):
    shape = (GRID_W, LANES)
    kc = lax.broadcasted_iota(jnp.int32, shape, 0)
    lane = lax.broadcasted_iota(jnp.int32, shape, 1)
    second_head = lane >= GRID_W
    c = jnp.where(second_head, lane - GRID_W, lane)
    start = jnp.clip(c - WIN_COLS // 2, 0, GRID_W - WIN_COLS)
    inside = (kc >= start) & (kc < start + WIN_COLS)
    for d in range(bias_scr.shape[0]):
        t0 = pltpu.roll(jnp.broadcast_to(rows_ref[0, d:d + 1, :], shape), LANES - (WIN_COLS - 1), 1,
                        stride=1, stride_axis=0)
        t1 = pltpu.roll(jnp.broadcast_to(rows_ref[1, d:d + 1, :], shape), GRID_W - (WIN_COLS - 1), 1,
                        stride=1, stride_axis=0)
        bias_scr[d] = jnp.where(inside, jnp.where(second_head, t1, t0), MASK_VALUE)


def _out_proj_kernel(ya_ref, yb_ref, h_ref, ga_ref, gb_ref, wa_ref, wb_ref, o_ref):
    bsz, ts, d = h_ref.shape
    ya = jnp.swapaxes(ya_ref[...], 0, 1).reshape(bsz * ts, -1)
    yb = yb_ref[...].reshape(bsz * ts, -1)
    ua = _rms_norm(ya, ga_ref[...]).astype(jnp.bfloat16)
    ub = _rms_norm(yb, gb_ref[...]).astype(jnp.bfloat16)
    out = (h_ref[...].reshape(bsz * ts, d)
           + jnp.dot(ua, wa_ref[...], preferred_element_type=jnp.float32)
           + jnp.dot(ub, wb_ref[...], preferred_element_type=jnp.float32))
    o_ref[...] = out.reshape(bsz, ts, d)


def _out_proj(ya, yb, h, ga, gb, w):
    bsz, s, d = h.shape
    ca, cb = ya.shape[2], yb.shape[2]
    ts = OUT_TM // bsz
    return pl.pallas_call(
        _out_proj_kernel,
        grid=(s // ts, 1),
        in_specs=[
            pl.BlockSpec((ts, bsz, ca), lambda i, j: (i, 0, 0)),
            pl.BlockSpec((bsz, ts, cb), lambda i, j: (0, i, 0)),
            pl.BlockSpec((bsz, ts, d), lambda i, j: (0, i, 0)),
            pl.BlockSpec((1, ca), lambda i, j: (0, 0)),
            pl.BlockSpec((1, cb), lambda i, j: (0, 0)),
            pl.BlockSpec((ca, d), lambda i, j: (0, 0)),
            pl.BlockSpec((cb, d), lambda i, j: (1, 0)),
        ],
        out_specs=pl.BlockSpec((bsz, ts, d), lambda i, j: (0, i, 0)),
        out_shape=jax.ShapeDtypeStruct((bsz, s, d), jnp.float32),
        compiler_params=_vmem(48),
        name="out_proj",
    )(ya, yb, h, ga, gb, w, w)


def kernel(x, norm_ffn1, ffn1_w_in, ffn1_w_out, norm_mix, w_in_mix, lru_conv_w, lru_conv_b, lru_gate_w, lru_gate_b, lru_lambda, attn_rpb, lru_out_norm, attn_out_norm, w_out_mix, norm_ffn2, ffn2_w_in, ffn2_w_out, norm_final):
    bsz, s, d = x.shape
    depth = norm_ffn1.shape[0]
    assert depth >= 1 and d == D_MODEL
    bf16 = jnp.bfloat16
    t = bsz * s
    h = x.reshape(t, d)
    final_gain = norm_final.reshape(1, d)
    for l in range(depth):
        h, w_in_mix_bf, w_out_mix_bf = _ffn(
            h, norm_ffn1[l].reshape(1, d), ffn1_w_in[l].astype(bf16), ffn1_w_out[l].astype(bf16),
            final_gain, final_norm=False, cast=(w_in_mix[l], w_out_mix[l]))

        h3 = h.reshape(bsz, s, d)
        xg, qkv = _in_proj(h3, norm_mix[l].reshape(1, d), w_in_mix_bf)

        gate_w, gate_b = _lru_gate_weights(lru_gate_w[l], lru_gate_b[l])
        ya = _lru(xg, gate_w, gate_b, lru_conv_w[l], lru_conv_b[l].reshape(1, LRU_WIDTH),
                  lru_lambda[l].reshape(2, 1, LRU_WIDTH))
        yb, ffn2_w_in_bf, ffn2_w_out_bf = _attention(qkv, _attn_bias_rows(attn_rpb[l]),
                                                     cast=(ffn2_w_in[l], ffn2_w_out[l]))

        h = _out_proj(ya, yb, h3, lru_out_norm[l].reshape(1, LRU_WIDTH),
                      attn_out_norm[l].reshape(1, ATTN_WIDTH), w_out_mix_bf).reshape(t, d)

        last = l == depth - 1
        h, = _ffn(h, norm_ffn2[l].reshape(1, d), ffn2_w_in_bf, ffn2_w_out_bf, final_gain, final_norm=last)
    return h.reshape(bsz, s, d)
```

```python
import functools

import jax
import jax.numpy as jnp
from jax import lax
from jax.experimental import pallas as pl
from jax.experimental.pallas import tpu as pltpu

D_MODEL = 2048
HEAD_DIM = 64
LRU_WIDTH = 1024
ATTN_WIDTH = 1024
CONV_WIDTH = 4
CONV_PAD_L = CONV_WIDTH // 2
RG_C = 8.0
GRID_W = 64
WIN_ROWS = 8
WIN_COLS = 16
D_FF = 5632
NORM_EPS = 1e-6

LANES = 128
SUBLANES = 8
BF16_SUBLANES = 16
MASK_VALUE = -1e30

FFN_TM = 1024
FFN_TF = 512
PROJ_TM = 1024
PROJ_TN = 1024
OUT_TM = 512
LRU_TB = 256
ATTN_SLOTS = 4


def _vmem(mib):
    return pltpu.CompilerParams(
        dimension_semantics=("arbitrary", "arbitrary"), vmem_limit_bytes=mib * 1024 * 1024)


def _rms_norm(x, gain):
    ms = jnp.mean(x * x, axis=-1, keepdims=True)
    return x * lax.rsqrt(ms + NORM_EPS) * gain


def _ffn_kernel(*refs, final_norm, n_cast):
    x_ref, g_ref, wg_ref, wu_ref, wo_ref, gf_ref = refs[:6]
    cast_src = refs[6:6 + n_cast]
    o_ref = refs[6 + n_cast]
    cast_dst = refs[7 + n_cast:7 + 2 * n_cast]
    u_ref = refs[7 + 2 * n_cast]
    k = pl.program_id(1)

    def step(first):
        _cast_blocks(cast_src, cast_dst)
        if first:
            u = _rms_norm(x_ref[...], g_ref[...]).astype(jnp.bfloat16)
            u_ref[...] = u
        else:
            u = u_ref[...]
        gate = jnp.dot(u, wg_ref[...], preferred_element_type=jnp.float32)
        up = jnp.dot(u, wu_ref[...], preferred_element_type=jnp.float32)
        act = (gate * jax.nn.sigmoid(gate)) * (0.5 * up)
        contrib = jnp.dot(act.astype(jnp.bfloat16), wo_ref[...], preferred_element_type=jnp.float32)
        o_ref[...] = (x_ref[...] if first else o_ref[...]) + contrib

    pl.when(k == 0)(functools.partial(step, True))
    pl.when(k > 0)(functools.partial(step, False))

    if final_norm:
        @pl.when(k == pl.num_programs(1) - 1)
        def _():
            o_ref[...] = _rms_norm(o_ref[...], gf_ref[...])


def _cast_specs(arrays, n_steps, flat_step):
    specs, shapes = [], []
    for w in arrays:
        rows = BF16_SUBLANES
        while w.shape[0] % rows or w.shape[0] // rows > n_steps:
            rows += BF16_SUBLANES
        specs.append(pl.BlockSpec(
            (rows, w.shape[1]), lambda *idx, nb=w.shape[0] // rows: (jnp.minimum(flat_step(*idx), nb - 1), 0)))
        shapes.append(jax.ShapeDtypeStruct(w.shape, jnp.bfloat16))
    return specs, shapes


def _cast_blocks(srcs, dsts):
    for src, dst in zip(srcs, dsts):
        dst[...] = src[...].astype(dst.dtype)


def _ffn(h, gain, w_in, w_out, final_gain, final_norm, cast=()):
    t, d = h.shape
    nk = D_FF // FFN_TF
    cast_specs, cast_shapes = _cast_specs(cast, (t // FFN_TM) * nk, lambda i, k: i * nk + k)
    return pl.pallas_call(
        functools.partial(_ffn_kernel, final_norm=final_norm, n_cast=len(cast)),
        grid=(t // FFN_TM, nk),
        in_specs=[
            pl.BlockSpec((FFN_TM, d), lambda i, k: (i, 0)),
            pl.BlockSpec((1, d), lambda i, k: (0, 0)),
            pl.BlockSpec((d, FFN_TF), lambda i, k: (0, k)),
            pl.BlockSpec((d, FFN_TF), lambda i, k: (0, k + nk)),
            pl.BlockSpec((FFN_TF, d), lambda i, k: (k, 0)),
            pl.BlockSpec((1, d), lambda i, k: (0, 0)),
        ] + cast_specs,
        out_specs=[pl.BlockSpec((FFN_TM, d), lambda i, k: (i, 0))] + cast_specs,
        out_shape=[jax.ShapeDtypeStruct((t, d), jnp.float32)] + cast_shapes,
        scratch_shapes=[pltpu.VMEM((FFN_TM, d), jnp.bfloat16)],
        compiler_params=_vmem(58),
        name="ffn_final" if final_norm else "ffn",
    )(h, gain, w_in, w_in, w_out, final_gain, *cast)


def _in_proj_kernel(x_ref, g_ref, w_ref, lru_ref, qkv_ref, u_ref, *, n_lru):
    n = pl.program_id(1)
    bsz, ts, d = x_ref.shape

    def step(first, time_major):
        if first:
            u = _rms_norm(x_ref[...].reshape(bsz * ts, d), g_ref[...]).astype(jnp.bfloat16)
            u_ref[...] = u
        else:
            u = u_ref[...]
        res = jnp.dot(u, w_ref[...], preferred_element_type=jnp.float32).reshape(bsz, ts, -1)
        if time_major:
            lru_ref[...] = jnp.swapaxes(res, 0, 1)
        else:
            qkv_ref[...] = res.astype(qkv_ref.dtype)

    pl.when(n == 0)(functools.partial(step, True, True))
    pl.when((n > 0) & (n < n_lru))(functools.partial(step, False, True))
    pl.when(n >= n_lru)(functools.partial(step, False, False))


def _in_proj(h, gain, w):
    bsz, s, d = h.shape
    ts = PROJ_TM // bsz
    n_lru = 2 * LRU_WIDTH // PROJ_TN
    n_qkv = 3 * ATTN_WIDTH // PROJ_TN
    return pl.pallas_call(
        functools.partial(_in_proj_kernel, n_lru=n_lru),
        grid=(s // ts, n_lru + n_qkv),
        in_specs=[
            pl.BlockSpec((bsz, ts, d), lambda i, n: (0, i, 0)),
            pl.BlockSpec((1, d), lambda i, n: (0, 0)),
            pl.BlockSpec((d, PROJ_TN), lambda i, n: (0, n)),
        ],
        out_specs=[
            pl.BlockSpec((ts, bsz, PROJ_TN), lambda i, n: (i, 0, jnp.minimum(n, n_lru - 1))),
            pl.BlockSpec((bsz, ts, PROJ_TN), lambda i, n: (0, i, jnp.maximum(n - n_lru, 0))),
        ],
        out_shape=[
            jax.ShapeDtypeStruct((s, bsz, 2 * LRU_WIDTH), jnp.float32),
            jax.ShapeDtypeStruct((bsz, s, 3 * ATTN_WIDTH), jnp.bfloat16),
        ],
        scratch_shapes=[pltpu.VMEM((PROJ_TM, d), jnp.bfloat16)],
        compiler_params=_vmem(56),
        name="in_proj",
    )(h, gain, w)


def _lru_kernel(x_ref, gf_ref, gr_ref, w_ref, gb_ref, cw_ref, cb_ref, lam_ref, o_ref,
                xw_scr, h_scr, *, n_chunks):
    j = pl.program_id(1)
    seq = x_ref.shape[0]
    rows = LRU_TB * SUBLANES
    n_tail = CONV_WIDTH - 1 - CONV_PAD_L

    @pl.when(j == 0)
    def _():
        h_scr[...] = jnp.zeros_like(h_scr)

    def visit(second):
        starts, coeffs, gates = [], [], []
        for d, g_ref in enumerate((gf_ref, gr_ref)):
            chunk = (n_chunks - 1 - j) if d else j
            t0 = pl.multiple_of(chunk * LRU_TB, LRU_TB)
            starts.append(t0)
            xw_scr[d, CONV_PAD_L:CONV_PAD_L + LRU_TB] = x_ref[pl.ds(t0, LRU_TB)]
            head = x_ref[pl.ds(jnp.maximum(t0 - CONV_PAD_L, 0), CONV_PAD_L)]
            xw_scr[d, :CONV_PAD_L] = jnp.where(chunk > 0, head, 0.0)
            tail = x_ref[pl.ds(jnp.minimum(t0 + LRU_TB, seq - n_tail), n_tail)]
            xw_scr[d, CONV_PAD_L + LRU_TB:CONV_PAD_L + LRU_TB + n_tail] = jnp.where(
                chunk < n_chunks - 1, tail, 0.0)

            xh = cw_ref[0:1, :][None] * xw_scr[d, 0:LRU_TB]
            for tap in range(1, CONV_WIDTH):
                xh = xh + cw_ref[tap:tap + 1, :][None] * xw_scr[d, tap:tap + LRU_TB]
            xh2 = xh.reshape(rows, LANES) + cb_ref[...]
            t = jnp.tanh(jnp.dot(xh2.astype(jnp.bfloat16), w_ref[d, 0], preferred_element_type=jnp.float32)
                         + gb_ref[d, 0])
            t_r, t_i = t[:, :LANES], t[:, LANES:]
            z = -lam_ref[d]
            softplus = jnp.maximum(z, 0.0) + jnp.log1p(jnp.exp(-jnp.abs(z)))
            half_c = (-0.5 * RG_C) * softplus
            log_a = half_c + half_c * t_r
            a = jnp.exp(log_a)
            sq = -jnp.tanh(log_a) * (a * a + 1.0)
            root = jnp.where(sq > 0.0, sq * lax.rsqrt(sq), 0.0)
            b = (root * xh2) * (1.0 + t_i)
            coeffs.append((a.reshape(LRU_TB, SUBLANES, LANES), b.reshape(LRU_TB, SUBLANES, LANES)))
            if second:
                gates.append((jax.nn.gelu(g_ref[...], approximate=True), o_ref[pl.ds(t0, LRU_TB)]))

        (a_f, b_f), (a_r, b_r) = coeffs
        hf, hr = h_scr[0], h_scr[1]
        for s in range(LRU_TB):
            tr = LRU_TB - 1 - s
            hf = a_f[s] * hf + b_f[s]
            hr = a_r[tr] * hr + b_r[tr]
            if second:
                o_ref[starts[0] + s] = gates[0][0][s] * (gates[0][1][s] + hf)
                o_ref[starts[1] + tr] = gates[1][0][tr] * (gates[1][1][tr] + hr)
            else:
                o_ref[starts[0] + s] = hf
                o_ref[starts[1] + tr] = hr
        h_scr[0] = hf
        h_scr[1] = hr

    pl.when(j < n_chunks // 2)(functools.partial(visit, False))
    pl.when(j >= n_chunks // 2)(functools.partial(visit, True))


def _lru(xg, gate_w, gate_b, conv_w, conv_b, lam):
    s, bsz, c2 = xg.shape
    c = c2 // 2
    n_cblk = c // LANES
    n_chunks = s // LRU_TB
    half = n_chunks // 2
    assert n_chunks % 2 == 0
    gate_fwd = pl.BlockSpec((LRU_TB, bsz, LANES), lambda cb, j: (jnp.maximum(j, half), 0, n_cblk + cb))
    gate_rev = pl.BlockSpec((LRU_TB, bsz, LANES),
                            lambda cb, j: (jnp.minimum(n_chunks - 1 - j, half - 1), 0, n_cblk + cb))
    return pl.pallas_call(
        functools.partial(_lru_kernel, n_chunks=n_chunks),
        grid=(n_cblk, n_chunks),
        in_specs=[
            pl.BlockSpec((s, bsz, LANES), lambda cb, j: (0, 0, cb)),
            gate_fwd,
            gate_rev,
            pl.BlockSpec((2, 1, LANES, 2 * LANES), lambda cb, j: (0, cb, 0, 0)),
            pl.BlockSpec((2, 1, 1, 2 * LANES), lambda cb, j: (0, cb, 0, 0)),
            pl.BlockSpec((CONV_WIDTH, LANES), lambda cb, j: (0, cb)),
            pl.BlockSpec((1, LANES), lambda cb, j: (0, cb)),
            pl.BlockSpec((2, 1, LANES), lambda cb, j: (0, 0, cb)),
        ],
        out_specs=pl.BlockSpec((s, bsz, LANES), lambda cb, j: (0, 0, cb)),
        out_shape=jax.ShapeDtypeStruct((s, bsz, c), jnp.float32),
        scratch_shapes=[
            pltpu.VMEM((2, LRU_TB + SUBLANES, bsz, LANES), jnp.float32),
            pltpu.VMEM((2, bsz, LANES), jnp.float32),
        ],
        compiler_params=_vmem(56),
        name="lru",
    )(xg, xg, xg, gate_w, 0.5 * gate_b, 0.5 * conv_w, 0.5 * conv_b, lam)


def _lru_gate_weights(gate_w, gate_b):
    n_cblk = LRU_WIDTH // LANES
    hpb = LANES // HEAD_DIM
    w = gate_w.reshape(2, 2, n_cblk, hpb, HEAD_DIM, HEAD_DIM)
    eye = jnp.eye(hpb, dtype=gate_w.dtype)
    wbd = w[:, :, :, :, :, None, :] * eye[None, None, None, :, None, :, None]
    wbd = wbd.reshape(2, 2, n_cblk, LANES, LANES)
    wbd = wbd.transpose(0, 2, 3, 1, 4).reshape(2, n_cblk, LANES, 2 * LANES)
    bias = gate_b.reshape(2, 2, n_cblk, LANES).transpose(0, 2, 1, 3).reshape(2, n_cblk, 1, 2 * LANES)
    return wbd.astype(jnp.bfloat16), bias


def _attn_kernel(*refs, rows, kr, n_cast):
    q_ref, k_ref, v_ref, rows_ref = refs[:4]
    o_ref = refs[4 + n_cast]
    q_scr, v_scr, p_scr, bias_scr = refs[5 + 2 * n_cast:]
    _cast_blocks(refs[4:4 + n_cast], refs[5 + n_cast:5 + 2 * n_cast])

    @pl.when(pl.program_id(1) == 0)
    def _():
        _build_bias_table(rows_ref, bias_scr)

    lane = lax.broadcasted_iota(jnp.int32, (1, 1, LANES), 2)
    first_head = lane < HEAD_DIM
    q = q_ref[0].reshape(rows, GRID_W, LANES) * HEAD_DIM ** -0.5
    for r in range(rows):
        stacked = jnp.concatenate([jnp.where(first_head[0], q[r], 0.0), jnp.where(first_head[0], 0.0, q[r])], axis=0)
        q_scr[r] = stacked.T
    v_scr[:, :LANES] = v_ref[0]
    v_scr[:, LANES:] = jnp.ones((rows * GRID_W, LANES), jnp.bfloat16)
    band = kr * GRID_W

    def band_start(r):
        return min(max(r - kr // 2, 0), rows - kr)

    def scores(r):
        rs = band_start(r)
        s_t = jnp.dot(k_ref[0, rs * GRID_W:rs * GRID_W + band, :], q_scr[r],
                      preferred_element_type=jnp.float32)
        s_t = s_t + jnp.concatenate([bias_scr[rs + i - r + WIN_ROWS - 1] for i in range(kr)], axis=0)
        return s_t

    def exponentials(r, s_t):
        m = jnp.max(s_t, axis=0, keepdims=True)
        p_scr[r % ATTN_SLOTS] = jnp.exp(s_t - m).astype(jnp.bfloat16)

    def weighted_values(r):
        rs = band_start(r)
        pv = lax.dot_general(p_scr[r % ATTN_SLOTS], v_scr[rs * GRID_W:rs * GRID_W + band, :],
                             (((0,), (0,)), ((), ())), preferred_element_type=jnp.float32)
        out = pv[:, :LANES] / pv[:, LANES:]
        o_ref[0, r * GRID_W:(r + 1) * GRID_W, :] = jnp.where(first_head[0], out[:GRID_W], out[GRID_W:])

    for step in range(rows + 2):
        if step >= 2:
            weighted_values(step - 2)
        if 1 <= step <= rows:
            exponentials(step - 1, pending)
        if step < rows:
            pending = scores(step)


def _attention(proj, bias, cast=()):
    bsz, s, _ = proj.shape
    rows = s // GRID_W
    kr = min(WIN_ROWS, rows)
    n_hp = ATTN_WIDTH // LANES
    q_blk = 0
    k_blk = q_blk + n_hp
    v_blk = k_blk + n_hp
    blk = (1, s, LANES)
    cast_specs, cast_shapes = _cast_specs(cast, n_hp * bsz, lambda hp, b: hp * bsz + b)
    return pl.pallas_call(
        functools.partial(_attn_kernel, rows=rows, kr=kr, n_cast=len(cast)),
        grid=(n_hp, bsz),
        in_specs=[
            pl.BlockSpec(blk, lambda hp, b: (b, 0, q_blk + hp)),
            pl.BlockSpec(blk, lambda hp, b: (b, 0, k_blk + hp)),
            pl.BlockSpec(blk, lambda hp, b: (b, 0, v_blk + hp)),
            pl.BlockSpec((2, 2 * WIN_ROWS - 1, LANES), lambda hp, b: (hp, 0, 0)),
        ] + cast_specs,
        out_specs=[pl.BlockSpec(blk, lambda hp, b: (b, 0, hp))] + cast_specs,
        out_shape=[jax.ShapeDtypeStruct((bsz, s, ATTN_WIDTH), jnp.float32)] + cast_shapes,
        scratch_shapes=[
            pltpu.VMEM((rows, 2 * GRID_W, LANES), jnp.bfloat16),
            pltpu.VMEM((s, 2 * LANES), jnp.bfloat16),
            pltpu.VMEM((ATTN_SLOTS, kr * GRID_W, LANES), jnp.bfloat16),
            pltpu.VMEM((2 * WIN_ROWS - 1, GRID_W, LANES), jnp.float32),
        ],
        compiler_params=_vmem(32),
        name="attn",
    )(proj, proj, proj, bias, *cast)


def _attn_bias_rows(rpb):
    return jnp.pad(rpb[:, :, ::-1], ((0, 0), (0, 0), (0, LANES - rpb.shape[2])), constant_values=MASK_VALUE)


def _build_bias_table(rows_ref, bias_scr):
    shape = (GRID_W, LANES)
    kc = lax.broadcasted_iota(jnp.int32, shape, 0)
    lane = lax.broadcasted_iota(jnp.int32, shape, 1)
    second_head = lane >= GRID_W
    c = jnp.where(second_head, lane - GRID_W, lane)
    start = jnp.clip(c - WIN_COLS // 2, 0, GRID_W - WIN_COLS)
    inside = (kc >= start) & (kc < start + WIN_COLS)
    for d in range(bias_scr.shape[0]):
        t0 = pltpu.roll(jnp.broadcast_to(rows_ref[0, d:d + 1, :], shape), LANES - (WIN_COLS - 1), 1,
                        stride=1, stride_axis=0)
        t1 = pltpu.roll(jnp.broadcast_to(rows_ref[1, d:d + 1, :], shape), GRID_W - (WIN_COLS - 1), 1,
                        stride=1, stride_axis=0)
        bias_scr[d] = jnp.where(inside, jnp.where(second_head, t1, t0), MASK_VALUE)


def _out_proj_kernel(ya_ref, yb_ref, h_ref, ga_ref, gb_ref, wa_ref, wb_ref, o_ref):
    bsz, ts, d = h_ref.shape
    ya = jnp.swapaxes(ya_ref[...], 0, 1).reshape(bsz * ts, -1)
    yb = yb_ref[...].reshape(bsz * ts, -1)
    ua = _rms_norm(ya, ga_ref[...]).astype(jnp.bfloat16)
    ub = _rms_norm(yb, gb_ref[...]).astype(jnp.bfloat16)
    out = (h_ref[...].reshape(bsz * ts, d)
           + jnp.dot(ua, wa_ref[...], preferred_element_type=jnp.float32)
           + jnp.dot(ub, wb_ref[...], preferred_element_type=jnp.float32))
    o_ref[...] = out.reshape(bsz, ts, d)


def _out_proj(ya, yb, h, ga, gb, w):
    bsz, s, d = h.shape
    ca, cb = ya.shape[2], yb.shape[2]
    ts = OUT_TM // bsz
    return pl.pallas_call(
        _out_proj_kernel,
        grid=(s // ts, 1),
        in_specs=[
            pl.BlockSpec((ts, bsz, ca), lambda i, j: (i, 0, 0)),
            pl.BlockSpec((bsz, ts, cb), lambda i, j: (0, i, 0)),
            pl.BlockSpec((bsz, ts, d), lambda i, j: (0, i, 0)),
            pl.BlockSpec((1, ca), lambda i, j: (0, 0)),
            pl.BlockSpec((1, cb), lambda i, j: (0, 0)),
            pl.BlockSpec((ca, d), lambda i, j: (0, 0)),
            pl.BlockSpec((cb, d), lambda i, j: (1, 0)),
        ],
        out_specs=pl.BlockSpec((bsz, ts, d), lambda i, j: (0, i, 0)),
        out_shape=jax.ShapeDtypeStruct((bsz, s, d), jnp.float32),
        compiler_params=_vmem(48),
        name="out_proj",
    )(ya, yb, h, ga, gb, w, w)


def kernel(x, norm_ffn1, ffn1_w_in, ffn1_w_out, norm_mix, w_in_mix, lru_conv_w, lru_conv_b, lru_gate_w, lru_gate_b, lru_lambda, attn_rpb, lru_out_norm, attn_out_norm, w_out_mix, norm_ffn2, ffn2_w_in, ffn2_w_out, norm_final):
    bsz, s, d = x.shape
    depth = norm_ffn1.shape[0]
    assert depth >= 1 and d == D_MODEL
    bf16 = jnp.bfloat16
    t = bsz * s
    h = x.reshape(t, d)
    final_gain = norm_final.reshape(1, d)
    for l in range(depth):
        h, w_in_mix_bf, w_out_mix_bf = _ffn(
            h, norm_ffn1[l].reshape(1, d), ffn1_w_in[l].astype(bf16), ffn1_w_out[l].astype(bf16),
            final_gain, final_norm=False, cast=(w_in_mix[l], w_out_mix[l]))

        h3 = h.reshape(bsz, s, d)
        xg, qkv = _in_proj(h3, norm_mix[l].reshape(1, d), w_in_mix_bf)

        gate_w, gate_b = _lru_gate_weights(lru_gate_w[l], lru_gate_b[l])
        ya = _lru(xg, gate_w, gate_b, lru_conv_w[l], lru_conv_b[l].reshape(1, LRU_WIDTH),
                  lru_lambda[l].reshape(2, 1, LRU_WIDTH))
        yb, ffn2_w_in_bf, ffn2_w_out_bf = _attention(qkv, _attn_bias_rows(attn_rpb[l]),
                                                     cast=(ffn2_w_in[l], ffn2_w_out[l]))

        h = _out_proj(ya, yb, h3, lru_out_norm[l].reshape(1, LRU_WIDTH),
                      attn_out_norm[l].reshape(1, ATTN_WIDTH), w_out_mix_bf).reshape(t, d)

        last = l == depth - 1
        h, = _ffn(h, norm_ffn2[l].reshape(1, d), ffn2_w_in_bf, ffn2_w_out_bf, final_gain, final_norm=last)
    return h.reshape(bsz, s, d)
```

```python
import functools

import jax
import jax.numpy as jnp
from jax import lax
from jax.experimental import pallas as pl
from jax.experimental.pallas import tpu as pltpu

D_MODEL = 2048
HEAD_DIM = 64
LRU_WIDTH = 1024
ATTN_WIDTH = 1024
CONV_WIDTH = 4
CONV_PAD_L = CONV_WIDTH // 2
RG_C = 8.0
GRID_W = 64
WIN_ROWS = 8
WIN_COLS = 16
D_FF = 5632
NORM_EPS = 1e-6
LOG2_E = 1.4426950408889634
GELU_C0 = 0.7978845608028654
GELU_C1 = GELU_C0 * 0.044715

LANES = 128
SUBLANES = 8
BF16_SUBLANES = 16
MASK_VALUE = -1e30

FFN_TM = 1024
FFN_TF = 512
PROJ_TM = 1024
PROJ_TN = 1024
OUT_TM = 512
LRU_TB = 256
ATTN_SLOTS = 4


def _vmem(mib):
    return pltpu.CompilerParams(
        dimension_semantics=("arbitrary", "arbitrary"), vmem_limit_bytes=mib * 1024 * 1024)


def _gelu_tanh(x):
    inner = x * (GELU_C0 + GELU_C1 * (x * x))
    half = 0.5 * x
    return half + half * jnp.tanh(inner)


def _rms_norm(x, gain):
    ms = jnp.mean(x * x, axis=-1, keepdims=True)
    return x * lax.rsqrt(ms + NORM_EPS) * gain


def _ffn_kernel(*refs, final_norm, n_cast):
    x_ref, g_ref, wg_ref, wu_ref, wo_ref, gf_ref = refs[:6]
    cast_src = refs[6:6 + n_cast]
    o_ref = refs[6 + n_cast]
    cast_dst = refs[7 + n_cast:7 + 2 * n_cast]
    u_ref = refs[7 + 2 * n_cast]
    k = pl.program_id(1)

    def step(first):
        _cast_blocks(cast_src, cast_dst)
        if first:
            u = _rms_norm(x_ref[...], g_ref[...]).astype(jnp.bfloat16)
            u_ref[...] = u
        else:
            u = u_ref[...]
        gate = jnp.dot(u, wg_ref[...], preferred_element_type=jnp.float32)
        up = jnp.dot(u, wu_ref[...], preferred_element_type=jnp.float32)
        act = (gate * jax.nn.sigmoid(gate)) * (0.5 * up)
        contrib = jnp.dot(act.astype(jnp.bfloat16), wo_ref[...], preferred_element_type=jnp.float32)
        o_ref[...] = (x_ref[...] if first else o_ref[...]) + contrib

    pl.when(k == 0)(functools.partial(step, True))
    pl.when(k > 0)(functools.partial(step, False))

    if final_norm:
        @pl.when(k == pl.num_programs(1) - 1)
        def _():
            o_ref[...] = _rms_norm(o_ref[...], gf_ref[...])


def _cast_specs(arrays, n_steps, flat_step):
    specs, shapes = [], []
    for w in arrays:
        rows = BF16_SUBLANES
        while w.shape[0] % rows or w.shape[0] // rows > n_steps:
            rows += BF16_SUBLANES
        specs.append(pl.BlockSpec(
            (rows, w.shape[1]), lambda *idx, nb=w.shape[0] // rows: (jnp.minimum(flat_step(*idx), nb - 1), 0)))
        shapes.append(jax.ShapeDtypeStruct(w.shape, jnp.bfloat16))
    return specs, shapes


def _cast_blocks(srcs, dsts):
    for src, dst in zip(srcs, dsts):
        dst[...] = src[...].astype(dst.dtype)


def _ffn(h, gain, w_in, w_out, final_gain, final_norm, cast=()):
    t, d = h.shape
    nk = D_FF // FFN_TF
    cast_specs, cast_shapes = _cast_specs(cast, (t // FFN_TM) * nk, lambda i, k: i * nk + k)
    return pl.pallas_call(
        functools.partial(_ffn_kernel, final_norm=final_norm, n_cast=len(cast)),
        grid=(t // FFN_TM, nk),
        in_specs=[
            pl.BlockSpec((FFN_TM, d), lambda i, k: (i, 0)),
            pl.BlockSpec((1, d), lambda i, k: (0, 0)),
            pl.BlockSpec((d, FFN_TF), lambda i, k: (0, k)),
            pl.BlockSpec((d, FFN_TF), lambda i, k: (0, k + nk)),
            pl.BlockSpec((FFN_TF, d), lambda i, k: (k, 0)),
            pl.BlockSpec((1, d), lambda i, k: (0, 0)),
        ] + cast_specs,
        out_specs=[pl.BlockSpec((FFN_TM, d), lambda i, k: (i, 0))] + cast_specs,
        out_shape=[jax.ShapeDtypeStruct((t, d), jnp.float32)] + cast_shapes,
        scratch_shapes=[pltpu.VMEM((FFN_TM, d), jnp.bfloat16)],
        compiler_params=_vmem(58),
        name="ffn_final" if final_norm else "ffn",
    )(h, gain, w_in, w_in, w_out, final_gain, *cast)


def _in_proj_kernel(x_ref, g_ref, w_ref, lru_ref, qkv_ref, u_ref, *, n_lru):
    n = pl.program_id(1)
    bsz, ts, d = x_ref.shape

    def step(first, time_major):
        if first:
            u = _rms_norm(x_ref[...].reshape(bsz * ts, d), g_ref[...]).astype(jnp.bfloat16)
            u_ref[...] = u
        else:
            u = u_ref[...]
        res = jnp.dot(u, w_ref[...], preferred_element_type=jnp.float32).reshape(bsz, ts, -1)
        if time_major:
            lru_ref[...] = jnp.swapaxes(res, 0, 1)
        else:
            qkv_ref[...] = res.astype(qkv_ref.dtype)

    pl.when(n == 0)(functools.partial(step, True, True))
    pl.when((n > 0) & (n < n_lru))(functools.partial(step, False, True))
    pl.when(n >= n_lru)(functools.partial(step, False, False))


def _in_proj(h, gain, w):
    bsz, s, d = h.shape
    ts = PROJ_TM // bsz
    n_lru = 2 * LRU_WIDTH // PROJ_TN
    n_qkv = 3 * ATTN_WIDTH // PROJ_TN
    return pl.pallas_call(
        functools.partial(_in_proj_kernel, n_lru=n_lru),
        grid=(s // ts, n_lru + n_qkv),
        in_specs=[
            pl.BlockSpec((bsz, ts, d), lambda i, n: (0, i, 0)),
            pl.BlockSpec((1, d), lambda i, n: (0, 0)),
            pl.BlockSpec((d, PROJ_TN), lambda i, n: (0, n)),
        ],
        out_specs=[
            pl.BlockSpec((ts, bsz, PROJ_TN), lambda i, n: (i, 0, jnp.minimum(n, n_lru - 1))),
            pl.BlockSpec((bsz, ts, PROJ_TN), lambda i, n: (0, i, jnp.maximum(n - n_lru, 0))),
        ],
        out_shape=[
            jax.ShapeDtypeStruct((s, bsz, 2 * LRU_WIDTH), jnp.float32),
            jax.ShapeDtypeStruct((bsz, s, 3 * ATTN_WIDTH), jnp.bfloat16),
        ],
        scratch_shapes=[pltpu.VMEM((PROJ_TM, d), jnp.bfloat16)],
        compiler_params=_vmem(56),
        name="in_proj",
    )(h, gain, w)


def _lru_kernel(x_ref, gf_ref, gr_ref, w_ref, gb_ref, cw_ref, cb_ref, lam_ref, o_ref,
                xw_scr, h_scr, *, n_chunks):
    j = pl.program_id(1)
    seq = x_ref.shape[0]
    rows = LRU_TB * SUBLANES
    n_tail = CONV_WIDTH - 1 - CONV_PAD_L

    @pl.when(j == 0)
    def _():
        h_scr[...] = jnp.zeros_like(h_scr)

    def visit(second):
        starts, coeffs, gates = [], [], []
        for d, g_ref in enumerate((gf_ref, gr_ref)):
            chunk = (n_chunks - 1 - j) if d else j
            t0 = pl.multiple_of(chunk * LRU_TB, LRU_TB)
            starts.append(t0)
            xw_scr[d, CONV_PAD_L:CONV_PAD_L + LRU_TB] = x_ref[pl.ds(t0, LRU_TB)]
            head = x_ref[pl.ds(jnp.maximum(t0 - CONV_PAD_L, 0), CONV_PAD_L)]
            xw_scr[d, :CONV_PAD_L] = jnp.where(chunk > 0, head, 0.0)
            tail = x_ref[pl.ds(jnp.minimum(t0 + LRU_TB, seq - n_tail), n_tail)]
            xw_scr[d, CONV_PAD_L + LRU_TB:CONV_PAD_L + LRU_TB + n_tail] = jnp.where(
                chunk < n_chunks - 1, tail, 0.0)

            xh = cw_ref[0:1, :][None] * xw_scr[d, 0:LRU_TB]
            for tap in range(1, CONV_WIDTH):
                xh = xh + cw_ref[tap:tap + 1, :][None] * xw_scr[d, tap:tap + LRU_TB]
            xh2 = xh.reshape(rows, LANES) + cb_ref[...]
            t = jnp.tanh(jnp.dot(xh2.astype(jnp.bfloat16), w_ref[d, 0], preferred_element_type=jnp.float32)
                         + gb_ref[d, 0])
            t_r, t_i = t[:, :LANES], t[:, LANES:]
            z = -lam_ref[d]
            softplus = jnp.maximum(z, 0.0) + jnp.log1p(jnp.exp(-jnp.abs(z)))
            half_c = (0.5 * RG_C) * softplus
            neg_log_a = half_c + half_c * t_r
            a = jnp.exp2(neg_log_a * -LOG2_E)
            sq = jnp.tanh(neg_log_a) * (a * a + 1.0)
            root = jnp.where(sq > 0.0, sq * lax.rsqrt(sq), 0.0)
            b = (root * xh2) * (1.0 + t_i)
            coeffs.append((a.reshape(LRU_TB, SUBLANES, LANES), b.reshape(LRU_TB, SUBLANES, LANES)))
            if second:
                gates.append((_gelu_tanh(g_ref[...]), o_ref[pl.ds(t0, LRU_TB)]))

        (a_f, b_f), (a_r, b_r) = coeffs
        hf, hr = h_scr[0], h_scr[1]
        for s in range(LRU_TB):
            tr = LRU_TB - 1 - s
            hf = a_f[s] * hf + b_f[s]
            hr = a_r[tr] * hr + b_r[tr]
            if second:
                o_ref[starts[0] + s] = gates[0][0][s] * (gates[0][1][s] + hf)
                o_ref[starts[1] + tr] = gates[1][0][tr] * (gates[1][1][tr] + hr)
            else:
                o_ref[starts[0] + s] = hf
                o_ref[starts[1] + tr] = hr
        h_scr[0] = hf
        h_scr[1] = hr

    pl.when(j < n_chunks // 2)(functools.partial(visit, False))
    pl.when(j >= n_chunks // 2)(functools.partial(visit, True))


def _lru(xg, gate_w, gate_b, conv_w, conv_b, lam):
    s, bsz, c2 = xg.shape
    c = c2 // 2
    n_cblk = c // LANES
    n_chunks = s // LRU_TB
    half = n_chunks // 2
    assert n_chunks % 2 == 0
    gate_fwd = pl.BlockSpec((LRU_TB, bsz, LANES), lambda cb, j: (jnp.maximum(j, half), 0, n_cblk + cb))
    gate_rev = pl.BlockSpec((LRU_TB, bsz, LANES),
                            lambda cb, j: (jnp.minimum(n_chunks - 1 - j, half - 1), 0, n_cblk + cb))
    return pl.pallas_call(
        functools.partial(_lru_kernel, n_chunks=n_chunks),
        grid=(n_cblk, n_chunks),
        in_specs=[
            pl.BlockSpec((s, bsz, LANES), lambda cb, j: (0, 0, cb)),
            gate_fwd,
            gate_rev,
            pl.BlockSpec((2, 1, LANES, 2 * LANES), lambda cb, j: (0, cb, 0, 0)),
            pl.BlockSpec((2, 1, 1, 2 * LANES), lambda cb, j: (0, cb, 0, 0)),
            pl.BlockSpec((CONV_WIDTH, LANES), lambda cb, j: (0, cb)),
            pl.BlockSpec((1, LANES), lambda cb, j: (0, cb)),
            pl.BlockSpec((2, 1, LANES), lambda cb, j: (0, 0, cb)),
        ],
        out_specs=pl.BlockSpec((s, bsz, LANES), lambda cb, j: (0, 0, cb)),
        out_shape=jax.ShapeDtypeStruct((s, bsz, c), jnp.float32),
        scratch_shapes=[
            pltpu.VMEM((2, LRU_TB + SUBLANES, bsz, LANES), jnp.float32),
            pltpu.VMEM((2, bsz, LANES), jnp.float32),
        ],
        compiler_params=_vmem(56),
        name="lru",
    )(xg, xg, xg, gate_w, 0.5 * gate_b, 0.5 * conv_w, 0.5 * conv_b, lam)


def _lru_gate_weights(gate_w, gate_b):
    n_cblk = LRU_WIDTH // LANES
    hpb = LANES // HEAD_DIM
    w = gate_w.reshape(2, 2, n_cblk, hpb, HEAD_DIM, HEAD_DIM)
    eye = jnp.eye(hpb, dtype=gate_w.dtype)
    wbd = w[:, :, :, :, :, None, :] * eye[None, None, None, :, None, :, None]
    wbd = wbd.reshape(2, 2, n_cblk, LANES, LANES)
    wbd = wbd.transpose(0, 2, 3, 1, 4).reshape(2, n_cblk, LANES, 2 * LANES)
    bias = gate_b.reshape(2, 2, n_cblk, LANES).transpose(0, 2, 1, 3).reshape(2, n_cblk, 1, 2 * LANES)
    return wbd.astype(jnp.bfloat16), bias


def _attn_kernel(*refs, rows, kr, n_cast):
    q_ref, k_ref, v_ref, rows_ref = refs[:4]
    o_ref = refs[4 + n_cast]
    q_scr, v_scr, p_scr, bias_scr = refs[5 + 2 * n_cast:]
    _cast_blocks(refs[4:4 + n_cast], refs[5 + n_cast:5 + 2 * n_cast])

    @pl.when(pl.program_id(1) == 0)
    def _():
        _build_bias_table(rows_ref, bias_scr)

    lane = lax.broadcasted_iota(jnp.int32, (1, 1, LANES), 2)
    first_head = lane < HEAD_DIM
    q = q_ref[0].reshape(rows, GRID_W, LANES) * HEAD_DIM ** -0.5
    for r in range(rows):
        stacked = jnp.concatenate([jnp.where(first_head[0], q[r], 0.0), jnp.where(first_head[0], 0.0, q[r])], axis=0)
        q_scr[r] = stacked.T
    v_scr[:, :LANES] = v_ref[0]
    v_scr[:, LANES:] = jnp.ones((rows * GRID_W, LANES), jnp.bfloat16)
    band = kr * GRID_W

    def band_start(r):
        return min(max(r - kr // 2, 0), rows - kr)

    def scores(r):
        rs = band_start(r)
        s_t = jnp.dot(k_ref[0, rs * GRID_W:rs * GRID_W + band, :], q_scr[r],
                      preferred_element_type=jnp.float32)
        s_t = s_t + jnp.concatenate([bias_scr[rs + i - r + WIN_ROWS - 1] for i in range(kr)], axis=0)
        return s_t

    def exponentials(r, s_t):
        m = jnp.max(s_t, axis=0, keepdims=True)
        p_scr[r % ATTN_SLOTS] = jnp.exp(s_t - m).astype(jnp.bfloat16)

    def weighted_values(r):
        rs = band_start(r)
        pv = lax.dot_general(p_scr[r % ATTN_SLOTS], v_scr[rs * GRID_W:rs * GRID_W + band, :],
                             (((0,), (0,)), ((), ())), preferred_element_type=jnp.float32)
        out = pv[:, :LANES] / pv[:, LANES:]
        o_ref[0, r * GRID_W:(r + 1) * GRID_W, :] = jnp.where(first_head[0], out[:GRID_W], out[GRID_W:])

    for step in range(rows + 2):
        if step >= 2:
            weighted_values(step - 2)
        if 1 <= step <= rows:
            exponentials(step - 1, pending)
        if step < rows:
            pending = scores(step)


def _attention(proj, bias, cast=()):
    bsz, s, _ = proj.shape
    rows = s // GRID_W
    kr = min(WIN_ROWS, rows)
    n_hp = ATTN_WIDTH // LANES
    q_blk = 0
    k_blk = q_blk + n_hp
    v_blk = k_blk + n_hp
    blk = (1, s, LANES)
    cast_specs, cast_shapes = _cast_specs(cast, n_hp * bsz, lambda hp, b: hp * bsz + b)
    return pl.pallas_call(
        functools.partial(_attn_kernel, rows=rows, kr=kr, n_cast=len(cast)),
        grid=(n_hp, bsz),
        in_specs=[
            pl.BlockSpec(blk, lambda hp, b: (b, 0, q_blk + hp)),
            pl.BlockSpec(blk, lambda hp, b: (b, 0, k_blk + hp)),
            pl.BlockSpec(blk, lambda hp, b: (b, 0, v_blk + hp)),
            pl.BlockSpec((2, 2 * WIN_ROWS - 1, LANES), lambda hp, b: (hp, 0, 0)),
        ] + cast_specs,
        out_specs=[pl.BlockSpec(blk, lambda hp, b: (b, 0, hp))] + cast_specs,
        out_shape=[jax.ShapeDtypeStruct((bsz, s, ATTN_WIDTH), jnp.float32)] + cast_shapes,
        scratch_shapes=[
            pltpu.VMEM((rows, 2 * GRID_W, LANES), jnp.bfloat16),
            pltpu.VMEM((s, 2 * LANES), jnp.bfloat16),
            pltpu.VMEM((ATTN_SLOTS, kr * GRID_W, LANES), jnp.bfloat16),
            pltpu.VMEM((2 * WIN_ROWS - 1, GRID_W, LANES), jnp.float32),
        ],
        compiler_params=_vmem(32),
        name="attn",
    )(proj, proj, proj, bias, *cast)


def _attn_bias_rows(rpb):
    return jnp.pad(rpb[:, :, ::-1], ((0, 0), (0, 0), (0, LANES - rpb.shape[2])), constant_values=MASK_VALUE)


def _build_bias_table(rows_ref, bias_scr):
    shape = (GRID_W, LANES)
    kc = lax.broadcasted_iota(jnp.int32, shape, 0)
    lane = lax.broadcasted_iota(jnp.int32, shape, 1)
    second_head = lane >= GRID_W
    c = jnp.where(second_head, lane - GRID_W, lane)
    start = jnp.clip(c - WIN_COLS // 2, 0, GRID_W - WIN_COLS)
    inside = (kc >= start) & (kc < start + WIN_COLS)
    for d in range(bias_scr.shape[0]):
        t0 = pltpu.roll(jnp.broadcast_to(rows_ref[0, d:d + 1, :], shape), LANES - (WIN_COLS - 1), 1,
                        stride=1, stride_axis=0)
        t1 = pltpu.roll(jnp.broadcast_to(rows_ref[1, d:d + 1, :], shape), GRID_W - (WIN_COLS - 1), 1,
                        stride=1, stride_axis=0)
        bias_scr[d] = jnp.where(inside, jnp.where(second_head, t1, t0), MASK_VALUE)


def _out_proj_kernel(ya_ref, yb_ref, h_ref, ga_ref, gb_ref, wa_ref, wb_ref, o_ref):
    bsz, ts, d = h_ref.shape
    ya = jnp.swapaxes(ya_ref[...], 0, 1).reshape(bsz * ts, -1)
    yb = yb_ref[...].reshape(bsz * ts, -1)
    ua = _rms_norm(ya, ga_ref[...]).astype(jnp.bfloat16)
    ub = _rms_norm(yb, gb_ref[...]).astype(jnp.bfloat16)
    out = (h_ref[...].reshape(bsz * ts, d)
           + jnp.dot(ua, wa_ref[...], preferred_element_type=jnp.float32)
           + jnp.dot(ub, wb_ref[...], preferred_element_type=jnp.float32))
    o_ref[...] = out.reshape(bsz, ts, d)


def _out_proj(ya, yb, h, ga, gb, w):
    bsz, s, d = h.shape
    ca, cb = ya.shape[2], yb.shape[2]
    ts = OUT_TM // bsz
    return pl.pallas_call(
        _out_proj_kernel,
        grid=(s // ts, 1),
        in_specs=[
            pl.BlockSpec((ts, bsz, ca), lambda i, j: (i, 0, 0)),
            pl.BlockSpec((bsz, ts, cb), lambda i, j: (0, i, 0)),
            pl.BlockSpec((bsz, ts, d), lambda i, j: (0, i, 0)),
            pl.BlockSpec((1, ca), lambda i, j: (0, 0)),
            pl.BlockSpec((1, cb), lambda i, j: (0, 0)),
            pl.BlockSpec((ca, d), lambda i, j: (0, 0)),
            pl.BlockSpec((cb, d), lambda i, j: (1, 0)),
        ],
        out_specs=pl.BlockSpec((bsz, ts, d), lambda i, j: (0, i, 0)),
        out_shape=jax.ShapeDtypeStruct((bsz, s, d), jnp.float32),
        compiler_params=_vmem(48),
        name="out_proj",
    )(ya, yb, h, ga, gb, w, w)


def kernel(x, norm_ffn1, ffn1_w_in, ffn1_w_out, norm_mix, w_in_mix, lru_conv_w, lru_conv_b, lru_gate_w, lru_gate_b, lru_lambda, attn_rpb, lru_out_norm, attn_out_norm, w_out_mix, norm_ffn2, ffn2_w_in, ffn2_w_out, norm_final):
    bsz, s, d = x.shape
    depth = norm_ffn1.shape[0]
    assert depth >= 1 and d == D_MODEL
    bf16 = jnp.bfloat16
    t = bsz * s
    h = x.reshape(t, d)
    final_gain = norm_final.reshape(1, d)
    for l in range(depth):
        h, w_in_mix_bf, w_out_mix_bf = _ffn(
            h, norm_ffn1[l].reshape(1, d), ffn1_w_in[l].astype(bf16), ffn1_w_out[l].astype(bf16),
            final_gain, final_norm=False, cast=(w_in_mix[l], w_out_mix[l]))

        h3 = h.reshape(bsz, s, d)
        xg, qkv = _in_proj(h3, norm_mix[l].reshape(1, d), w_in_mix_bf)

        gate_w, gate_b = _lru_gate_weights(lru_gate_w[l], lru_gate_b[l])
        ya = _lru(xg, gate_w, gate_b, lru_conv_w[l], lru_conv_b[l].reshape(1, LRU_WIDTH),
                  lru_lambda[l].reshape(2, 1, LRU_WIDTH))
        yb, ffn2_w_in_bf, ffn2_w_out_bf = _attention(qkv, _attn_bias_rows(attn_rpb[l]),
                                                     cast=(ffn2_w_in[l], ffn2_w_out[l]))

        h = _out_proj(ya, yb, h3, lru_out_norm[l].reshape(1, LRU_WIDTH),
                      attn_out_norm[l].reshape(1, ATTN_WIDTH), w_out_mix_bf).reshape(t, d)

        last = l == depth - 1
        h, = _ffn(h, norm_ffn2[l].reshape(1, d), ffn2_w_in_bf, ffn2_w_out_bf, final_gain, final_norm=last)
    return h.reshape(bsz, s, d)
```

```python
import functools

import jax
import jax.numpy as jnp
from jax import lax
from jax.experimental import pallas as pl
from jax.experimental.pallas import tpu as pltpu

D_MODEL = 2048
HEAD_DIM = 64
LRU_WIDTH = 1024
ATTN_WIDTH = 1024
CONV_WIDTH = 4
CONV_PAD_L = CONV_WIDTH // 2
RG_C = 8.0
GRID_W = 64
WIN_ROWS = 8
WIN_COLS = 16
D_FF = 5632
NORM_EPS = 1e-6
LOG2_E = 1.4426950408889634
GELU_C0 = 0.7978845608028654
GELU_C1 = GELU_C0 * 0.044715

LANES = 128
SUBLANES = 8
BF16_SUBLANES = 16
MASK_VALUE = -1e30

FFN_TM = 1024
FFN_TF = 512
FINAL_NORM_ROWS = 256
PROJ_TM = 1024
PROJ_TN = 1024
OUT_TM = 512
LRU_TB = 256
ATTN_SLOTS = 4


def _vmem(mib):
    return pltpu.CompilerParams(
        dimension_semantics=("arbitrary", "arbitrary"), vmem_limit_bytes=mib * 1024 * 1024)


def _gelu_tanh(x):
    inner = x * (GELU_C0 + GELU_C1 * (x * x))
    half = 0.5 * x
    return half + half * jnp.tanh(inner)


def _rms_norm(x, gain):
    ms = jnp.mean(x * x, axis=-1, keepdims=True)
    return x * lax.rsqrt(ms + NORM_EPS) * gain


def _ffn_kernel(*refs, final_norm, n_cast):
    x_ref, g_ref, wg_ref, wu_ref, wo_ref, gf_ref = refs[:6]
    cast_src = refs[6:6 + n_cast]
    o_ref = refs[6 + n_cast]
    cast_dst = refs[7 + n_cast:7 + 2 * n_cast]
    u_ref = refs[7 + 2 * n_cast]
    k = pl.program_id(1)

    def step(first, last):
        _cast_blocks(cast_src, cast_dst)
        if first:
            u = _rms_norm(x_ref[...], g_ref[...]).astype(jnp.bfloat16)
            u_ref[...] = u
        else:
            u = u_ref[...]
        gate = jnp.dot(u, wg_ref[...], preferred_element_type=jnp.float32)
        up = jnp.dot(u, wu_ref[...], preferred_element_type=jnp.float32)
        act = ((gate * jax.nn.sigmoid(gate)) * (0.5 * up)).astype(jnp.bfloat16)
        if last:
            for c in range(0, o_ref.shape[0], FINAL_NORM_ROWS):
                rows = slice(c, c + FINAL_NORM_ROWS)
                out = o_ref[rows] + jnp.dot(act[rows], wo_ref[...], preferred_element_type=jnp.float32)
                o_ref[rows] = _rms_norm(out, gf_ref[...])
        else:
            contrib = jnp.dot(act, wo_ref[...], preferred_element_type=jnp.float32)
            o_ref[...] = (x_ref[...] if first else o_ref[...]) + contrib

    n_k = pl.num_programs(1)
    pl.when(k == 0)(functools.partial(step, True, False))
    if final_norm:
        pl.when((k > 0) & (k < n_k - 1))(functools.partial(step, False, False))
        pl.when(k == n_k - 1)(functools.partial(step, False, True))
    else:
        pl.when(k > 0)(functools.partial(step, False, False))


def _cast_specs(arrays, n_steps, flat_step):
    specs, shapes = [], []
    for w in arrays:
        rows = BF16_SUBLANES
        while w.shape[0] % rows or w.shape[0] // rows > n_steps:
            rows += BF16_SUBLANES
        specs.append(pl.BlockSpec(
            (rows, w.shape[1]), lambda *idx, nb=w.shape[0] // rows: (jnp.minimum(flat_step(*idx), nb - 1), 0)))
        shapes.append(jax.ShapeDtypeStruct(w.shape, jnp.bfloat16))
    return specs, shapes


def _cast_blocks(srcs, dsts):
    for src, dst in zip(srcs, dsts):
        dst[...] = src[...].astype(dst.dtype)


def _ffn(h, gain, w_in, w_out, final_gain, final_norm, cast=()):
    t, d = h.shape
    nk = D_FF // FFN_TF
    cast_specs, cast_shapes = _cast_specs(cast, (t // FFN_TM) * nk, lambda i, k: i * nk + k)
    return pl.pallas_call(
        functools.partial(_ffn_kernel, final_norm=final_norm, n_cast=len(cast)),
        grid=(t // FFN_TM, nk),
        in_specs=[
            pl.BlockSpec((FFN_TM, d), lambda i, k: (i, 0)),
            pl.BlockSpec((1, d), lambda i, k: (0, 0)),
            pl.BlockSpec((d, FFN_TF), lambda i, k: (0, k)),
            pl.BlockSpec((d, FFN_TF), lambda i, k: (0, k + nk)),
            pl.BlockSpec((FFN_TF, d), lambda i, k: (k, 0)),
            pl.BlockSpec((1, d), lambda i, k: (0, 0)),
        ] + cast_specs,
        out_specs=[pl.BlockSpec((FFN_TM, d), lambda i, k: (i, 0))] + cast_specs,
        out_shape=[jax.ShapeDtypeStruct((t, d), jnp.float32)] + cast_shapes,
        scratch_shapes=[pltpu.VMEM((FFN_TM, d), jnp.bfloat16)],
        compiler_params=_vmem(58),
        name="ffn_final" if final_norm else "ffn",
    )(h, gain, w_in, w_in, w_out, final_gain, *cast)


def _in_proj_kernel(x_ref, g_ref, w_ref, lru_ref, qkv_ref, u_ref, *, n_lru):
    n = pl.program_id(1)
    bsz, ts, d = x_ref.shape

    def step(first, time_major):
        if first:
            u = _rms_norm(x_ref[...].reshape(bsz * ts, d), g_ref[...]).astype(jnp.bfloat16)
            u_ref[...] = u
        else:
            u = u_ref[...]
        res = jnp.dot(u, w_ref[...], preferred_element_type=jnp.float32).reshape(bsz, ts, -1)
        if time_major:
            lru_ref[...] = jnp.swapaxes(res, 0, 1)
        else:
            qkv_ref[...] = res.astype(qkv_ref.dtype)

    pl.when(n == 0)(functools.partial(step, True, True))
    pl.when((n > 0) & (n < n_lru))(functools.partial(step, False, True))
    pl.when(n >= n_lru)(functools.partial(step, False, False))


def _in_proj(h, gain, w):
    bsz, s, d = h.shape
    ts = PROJ_TM // bsz
    n_lru = 2 * LRU_WIDTH // PROJ_TN
    n_qkv = 3 * ATTN_WIDTH // PROJ_TN
    return pl.pallas_call(
        functools.partial(_in_proj_kernel, n_lru=n_lru),
        grid=(s // ts, n_lru + n_qkv),
        in_specs=[
            pl.BlockSpec((bsz, ts, d), lambda i, n: (0, i, 0)),
            pl.BlockSpec((1, d), lambda i, n: (0, 0)),
            pl.BlockSpec((d, PROJ_TN), lambda i, n: (0, n)),
        ],
        out_specs=[
            pl.BlockSpec((ts, bsz, PROJ_TN), lambda i, n: (i, 0, jnp.minimum(n, n_lru - 1))),
            pl.BlockSpec((bsz, ts, PROJ_TN), lambda i, n: (0, i, jnp.maximum(n - n_lru, 0))),
        ],
        out_shape=[
            jax.ShapeDtypeStruct((s, bsz, 2 * LRU_WIDTH), jnp.float32),
            jax.ShapeDtypeStruct((bsz, s, 3 * ATTN_WIDTH), jnp.bfloat16),
        ],
        scratch_shapes=[pltpu.VMEM((PROJ_TM, d), jnp.bfloat16)],
        compiler_params=_vmem(56),
        name="in_proj",
    )(h, gain, w)


def _lru_kernel(x_ref, gf_ref, gr_ref, w_ref, gb_ref, cw_ref, cb_ref, lam_ref, o_ref,
                xw_scr, h_scr, *, n_chunks):
    j = pl.program_id(1)
    seq = x_ref.shape[0]
    rows = LRU_TB * SUBLANES
    n_tail = CONV_WIDTH - 1 - CONV_PAD_L

    @pl.when(j == 0)
    def _():
        h_scr[...] = jnp.zeros_like(h_scr)

    def visit(second):
        starts, coeffs, gates = [], [], []
        for d, g_ref in enumerate((gf_ref, gr_ref)):
            chunk = (n_chunks - 1 - j) if d else j
            t0 = pl.multiple_of(chunk * LRU_TB, LRU_TB)
            starts.append(t0)
            xw_scr[d, CONV_PAD_L:CONV_PAD_L + LRU_TB] = x_ref[pl.ds(t0, LRU_TB)]
            head = x_ref[pl.ds(jnp.maximum(t0 - CONV_PAD_L, 0), CONV_PAD_L)]
            xw_scr[d, :CONV_PAD_L] = jnp.where(chunk > 0, head, 0.0)
            tail = x_ref[pl.ds(jnp.minimum(t0 + LRU_TB, seq - n_tail), n_tail)]
            xw_scr[d, CONV_PAD_L + LRU_TB:CONV_PAD_L + LRU_TB + n_tail] = jnp.where(
                chunk < n_chunks - 1, tail, 0.0)

            xh = cw_ref[0:1, :][None] * xw_scr[d, 0:LRU_TB]
            for tap in range(1, CONV_WIDTH):
                xh = xh + cw_ref[tap:tap + 1, :][None] * xw_scr[d, tap:tap + LRU_TB]
            xh2 = xh.reshape(rows, LANES) + cb_ref[...]
            t = jnp.tanh(jnp.dot(xh2.astype(jnp.bfloat16), w_ref[d, 0], preferred_element_type=jnp.float32)
                         + gb_ref[d, 0])
            t_r, t_i = t[:, :LANES], t[:, LANES:]
            z = -lam_ref[d]
            softplus = jnp.maximum(z, 0.0) + jnp.log1p(jnp.exp(-jnp.abs(z)))
            half_c = (0.5 * RG_C) * softplus
            neg_log_a = half_c + half_c * t_r
            a = jnp.exp2(neg_log_a * -LOG2_E)
            sq = jnp.tanh(neg_log_a) * (a * a + 1.0)
            root = jnp.where(sq > 0.0, sq * lax.rsqrt(sq), 0.0)
            b = (root * xh2) * (1.0 + t_i)
            coeffs.append((a.reshape(LRU_TB, SUBLANES, LANES), b.reshape(LRU_TB, SUBLANES, LANES)))
            if second:
                gates.append((_gelu_tanh(g_ref[...]), o_ref[pl.ds(t0, LRU_TB)]))

        (a_f, b_f), (a_r, b_r) = coeffs
        hf, hr = h_scr[0], h_scr[1]
        for s in range(LRU_TB):
            tr = LRU_TB - 1 - s
            hf = a_f[s] * hf + b_f[s]
            hr = a_r[tr] * hr + b_r[tr]
            if second:
                o_ref[starts[0] + s] = gates[0][0][s] * (gates[0][1][s] + hf)
                o_ref[starts[1] + tr] = gates[1][0][tr] * (gates[1][1][tr] + hr)
            else:
                o_ref[starts[0] + s] = hf
                o_ref[starts[1] + tr] = hr
        h_scr[0] = hf
        h_scr[1] = hr

    pl.when(j < n_chunks // 2)(functools.partial(visit, False))
    pl.when(j >= n_chunks // 2)(functools.partial(visit, True))


def _lru(xg, gate_w, gate_b, conv_w, conv_b, lam):
    s, bsz, c2 = xg.shape
    c = c2 // 2
    n_cblk = c // LANES
    n_chunks = s // LRU_TB
    half = n_chunks // 2
    assert n_chunks % 2 == 0
    gate_fwd = pl.BlockSpec((LRU_TB, bsz, LANES), lambda cb, j: (jnp.maximum(j, half), 0, n_cblk + cb))
    gate_rev = pl.BlockSpec((LRU_TB, bsz, LANES),
                            lambda cb, j: (jnp.minimum(n_chunks - 1 - j, half - 1), 0, n_cblk + cb))
    return pl.pallas_call(
        functools.partial(_lru_kernel, n_chunks=n_chunks),
        grid=(n_cblk, n_chunks),
        in_specs=[
            pl.BlockSpec((s, bsz, LANES), lambda cb, j: (0, 0, cb)),
            gate_fwd,
            gate_rev,
            pl.BlockSpec((2, 1, LANES, 2 * LANES), lambda cb, j: (0, cb, 0, 0)),
            pl.BlockSpec((2, 1, 1, 2 * LANES), lambda cb, j: (0, cb, 0, 0)),
            pl.BlockSpec((CONV_WIDTH, LANES), lambda cb, j: (0, cb)),
            pl.BlockSpec((1, LANES), lambda cb, j: (0, cb)),
            pl.BlockSpec((2, 1, LANES), lambda cb, j: (0, 0, cb)),
        ],
        out_specs=pl.BlockSpec((s, bsz, LANES), lambda cb, j: (0, 0, cb)),
        out_shape=jax.ShapeDtypeStruct((s, bsz, c), jnp.float32),
        scratch_shapes=[
            pltpu.VMEM((2, LRU_TB + SUBLANES, bsz, LANES), jnp.float32),
            pltpu.VMEM((2, bsz, LANES), jnp.float32),
        ],
        compiler_params=_vmem(56),
        name="lru",
    )(xg, xg, xg, gate_w, 0.5 * gate_b, 0.5 * conv_w, 0.5 * conv_b, lam)


def _lru_gate_weights(gate_w, gate_b):
    n_cblk = LRU_WIDTH // LANES
    hpb = LANES // HEAD_DIM
    w = gate_w.reshape(2, 2, n_cblk, hpb, HEAD_DIM, HEAD_DIM)
    eye = jnp.eye(hpb, dtype=gate_w.dtype)
    wbd = w[:, :, :, :, :, None, :] * eye[None, None, None, :, None, :, None]
    wbd = wbd.reshape(2, 2, n_cblk, LANES, LANES)
    wbd = wbd.transpose(0, 2, 3, 1, 4).reshape(2, n_cblk, LANES, 2 * LANES)
    bias = gate_b.reshape(2, 2, n_cblk, LANES).transpose(0, 2, 1, 3).reshape(2, n_cblk, 1, 2 * LANES)
    return wbd.astype(jnp.bfloat16), bias


def _attn_kernel(*refs, rows, kr, n_cast):
    q_ref, k_ref, v_ref, rows_ref = refs[:4]
    o_ref = refs[4 + n_cast]
    q_scr, v_scr, p_scr, bias_scr = refs[5 + 2 * n_cast:]
    _cast_blocks(refs[4:4 + n_cast], refs[5 + n_cast:5 + 2 * n_cast])

    @pl.when(pl.program_id(1) == 0)
    def _():
        _build_bias_table(rows_ref, bias_scr)

    lane = lax.broadcasted_iota(jnp.int32, (1, 1, LANES), 2)
    first_head = lane < HEAD_DIM
    q = q_ref[0].reshape(rows, GRID_W, LANES) * HEAD_DIM ** -0.5
    for r in range(rows):
        stacked = jnp.concatenate([jnp.where(first_head[0], q[r], 0.0), jnp.where(first_head[0], 0.0, q[r])], axis=0)
        q_scr[r] = stacked.T
    v_scr[:, :LANES] = v_ref[0]
    v_scr[:, LANES:] = jnp.ones((rows * GRID_W, LANES), jnp.bfloat16)
    band = kr * GRID_W

    def band_start(r):
        return min(max(r - kr // 2, 0), rows - kr)

    def scores(r):
        rs = band_start(r)
        s_t = jnp.dot(k_ref[0, rs * GRID_W:rs * GRID_W + band, :], q_scr[r],
                      preferred_element_type=jnp.float32)
        s_t = s_t + jnp.concatenate([bias_scr[rs + i - r + WIN_ROWS - 1] for i in range(kr)], axis=0)
        return s_t

    def exponentials(r, s_t):
        m = jnp.max(s_t, axis=0, keepdims=True)
        p_scr[r % ATTN_SLOTS] = jnp.exp(s_t - m).astype(jnp.bfloat16)

    def weighted_values(r):
        rs = band_start(r)
        pv = lax.dot_general(p_scr[r % ATTN_SLOTS], v_scr[rs * GRID_W:rs * GRID_W + band, :],
                             (((0,), (0,)), ((), ())), preferred_element_type=jnp.float32)
        out = pv[:, :LANES] / pv[:, LANES:]
        o_ref[0, r * GRID_W:(r + 1) * GRID_W, :] = jnp.where(first_head[0], out[:GRID_W], out[GRID_W:])

    for step in range(rows + 2):
        if step >= 2:
            weighted_values(step - 2)
        if 1 <= step <= rows:
            exponentials(step - 1, pending)
        if step < rows:
            pending = scores(step)


def _attention(proj, bias, cast=()):
    bsz, s, _ = proj.shape
    rows = s // GRID_W
    kr = min(WIN_ROWS, rows)
    n_hp = ATTN_WIDTH // LANES
    q_blk = 0
    k_blk = q_blk + n_hp
    v_blk = k_blk + n_hp
    blk = (1, s, LANES)
    cast_specs, cast_shapes = _cast_specs(cast, n_hp * bsz, lambda hp, b: hp * bsz + b)
    return pl.pallas_call(
        functools.partial(_attn_kernel, rows=rows, kr=kr, n_cast=len(cast)),
        grid=(n_hp, bsz),
        in_specs=[
            pl.BlockSpec(blk, lambda hp, b: (b, 0, q_blk + hp)),
            pl.BlockSpec(blk, lambda hp, b: (b, 0, k_blk + hp)),
            pl.BlockSpec(blk, lambda hp, b: (b, 0, v_blk + hp)),
            pl.BlockSpec((2, 2 * WIN_ROWS - 1, LANES), lambda hp, b: (hp, 0, 0)),
        ] + cast_specs,
        out_specs=[pl.BlockSpec(blk, lambda hp, b: (b, 0, hp))] + cast_specs,
        out_shape=[jax.ShapeDtypeStruct((bsz, s, ATTN_WIDTH), jnp.float32)] + cast_shapes,
        scratch_shapes=[
            pltpu.VMEM((rows, 2 * GRID_W, LANES), jnp.bfloat16),
            pltpu.VMEM((s, 2 * LANES), jnp.bfloat16),
            pltpu.VMEM((ATTN_SLOTS, kr * GRID_W, LANES), jnp.bfloat16),
            pltpu.VMEM((2 * WIN_ROWS - 1, GRID_W, LANES), jnp.float32),
        ],
        compiler_params=_vmem(32),
        name="attn",
    )(proj, proj, proj, bias, *cast)


def _attn_bias_rows(rpb):
    return jnp.pad(rpb[:, :, ::-1], ((0, 0), (0, 0), (0, LANES - rpb.shape[2])), constant_values=MASK_VALUE)


def _build_bias_table(rows_ref, bias_scr):
    shape = (GRID_W, LANES)
    kc = lax.broadcasted_iota(jnp.int32, shape, 0)
    lane = lax.broadcasted_iota(jnp.int32, shape, 1)
    second_head = lane >= GRID_W
    c = jnp.where(second_head, lane - GRID_W, lane)
    start = jnp.clip(c - WIN_COLS // 2, 0, GRID_W - WIN_COLS)
    inside = (kc >= start) & (kc < start + WIN_COLS)
    for d in range(bias_scr.shape[0]):
        t0 = pltpu.roll(jnp.broadcast_to(rows_ref[0, d:d + 1, :], shape), LANES - (WIN_COLS - 1), 1,
                        stride=1, stride_axis=0)
        t1 = pltpu.roll(jnp.broadcast_to(rows_ref[1, d:d + 1, :], shape), GRID_W - (WIN_COLS - 1), 1,
                        stride=1, stride_axis=0)
        bias_scr[d] = jnp.where(inside, jnp.where(second_head, t1, t0), MASK_VALUE)


def _out_proj_kernel(ya_ref, yb_ref, h_ref, ga_ref, gb_ref, wa_ref, wb_ref, o_ref):
    bsz, ts, d = h_ref.shape
    ya = jnp.swapaxes(ya_ref[...], 0, 1).reshape(bsz * ts, -1)
    yb = yb_ref[...].reshape(bsz * ts, -1)
    ua = _rms_norm(ya, ga_ref[...]).astype(jnp.bfloat16)
    ub = _rms_norm(yb, gb_ref[...]).astype(jnp.bfloat16)
    out = (h_ref[...].reshape(bsz * ts, d)
           + jnp.dot(ua, wa_ref[...], preferred_element_type=jnp.float32)
           + jnp.dot(ub, wb_ref[...], preferred_element_type=jnp.float32))
    o_ref[...] = out.reshape(bsz, ts, d)


def _out_proj(ya, yb, h, ga, gb, w):
    bsz, s, d = h.shape
    ca, cb = ya.shape[2], yb.shape[2]
    ts = OUT_TM // bsz
    return pl.pallas_call(
        _out_proj_kernel,
        grid=(s // ts, 1),
        in_specs=[
            pl.BlockSpec((ts, bsz, ca), lambda i, j: (i, 0, 0)),
            pl.BlockSpec((bsz, ts, cb), lambda i, j: (0, i, 0)),
            pl.BlockSpec((bsz, ts, d), lambda i, j: (0, i, 0)),
            pl.BlockSpec((1, ca), lambda i, j: (0, 0)),
            pl.BlockSpec((1, cb), lambda i, j: (0, 0)),
            pl.BlockSpec((ca, d), lambda i, j: (0, 0)),
            pl.BlockSpec((cb, d), lambda i, j: (1, 0)),
        ],
        out_specs=pl.BlockSpec((bsz, ts, d), lambda i, j: (0, i, 0)),
        out_shape=jax.ShapeDtypeStruct((bsz, s, d), jnp.float32),
        compiler_params=_vmem(48),
        name="out_proj",
    )(ya, yb, h, ga, gb, w, w)


def kernel(x, norm_ffn1, ffn1_w_in, ffn1_w_out, norm_mix, w_in_mix, lru_conv_w, lru_conv_b, lru_gate_w, lru_gate_b, lru_lambda, attn_rpb, lru_out_norm, attn_out_norm, w_out_mix, norm_ffn2, ffn2_w_in, ffn2_w_out, norm_final):
    bsz, s, d = x.shape
    depth = norm_ffn1.shape[0]
    assert depth >= 1 and d == D_MODEL
    bf16 = jnp.bfloat16
    t = bsz * s
    h = x.reshape(t, d)
    final_gain = norm_final.reshape(1, d)
    for l in range(depth):
        h, w_in_mix_bf, w_out_mix_bf = _ffn(
            h, norm_ffn1[l].reshape(1, d), ffn1_w_in[l].astype(bf16), ffn1_w_out[l].astype(bf16),
            final_gain, final_norm=False, cast=(w_in_mix[l], w_out_mix[l]))

        h3 = h.reshape(bsz, s, d)
        xg, qkv = _in_proj(h3, norm_mix[l].reshape(1, d), w_in_mix_bf)

        gate_w, gate_b = _lru_gate_weights(lru_gate_w[l], lru_gate_b[l])
        ya = _lru(xg, gate_w, gate_b, lru_conv_w[l], lru_conv_b[l].reshape(1, LRU_WIDTH),
                  lru_lambda[l].reshape(2, 1, LRU_WIDTH))
        yb, ffn2_w_in_bf, ffn2_w_out_bf = _attention(qkv, _attn_bias_rows(attn_rpb[l]),
                                                     cast=(ffn2_w_in[l], ffn2_w_out[l]))

        h = _out_proj(ya, yb, h3, lru_out_norm[l].reshape(1, LRU_WIDTH),
                      attn_out_norm[l].reshape(1, ATTN_WIDTH), w_out_mix_bf).reshape(t, d)

        last = l == depth - 1
        h, = _ffn(h, norm_ffn2[l].reshape(1, d), ffn2_w_in_bf, ffn2_w_out_bf, final_gain, final_norm=last)
    return h.reshape(bsz, s, d)
```
